```python
import jax, jax.numpy as jnp
from jax import lax
import numpy as np

D_MODEL = 1024
BATCH = 16
SEQ = 2048
DEPTH = 1

CHUNK = 64
LEFT_CHUNKS = 8
BAND = (LEFT_CHUNKS + 1) * CHUNK
REL_CLIP = 128
REL_TABLE = CHUNK + REL_CLIP

RWKV_HEAD_DIM = 64
RWKV_WIDTH = D_MODEL
RWKV_HEADS = RWKV_WIDTH // RWKV_HEAD_DIM
DECAY_LORA = 64
ICLR_LORA = 64
GATE_LORA = 160
RWKV_IN = 3 * RWKV_WIDTH + DECAY_LORA + ICLR_LORA + GATE_LORA

ATT_HEAD_DIM = 64
ATT_WIDTH = D_MODEL
ATT_HEADS = ATT_WIDTH // ATT_HEAD_DIM

IN_WIDTH = RWKV_IN + 3 * ATT_WIDTH + 2 * D_MODEL

MEM_TOKENS = 256
MEM_HEADS = 4
MEM_WIDTH = D_MODEL // 2
MEM_HEAD_DIM = MEM_WIDTH // MEM_HEADS

FFN_HIDDEN = ((8 * D_MODEL) // 3 + 255) // 256 * 256

NORM_EPS = 1e-6
GROUP_NORM_EPS = 64e-5
MASK_VALUE = -1e30

kernel_name = "hybrid_rwkv7_chunkattn_gated_block"


def rms_norm(x, gain):
    xf = x.astype(jnp.float32)
    y = xf * lax.rsqrt(jnp.mean(xf * xf, axis=-1, keepdims=True) + NORM_EPS)
    return (y * gain.astype(jnp.float32)).astype(x.dtype)


def token_shift(p):
    return jnp.pad(p, ((0, 0), (1, 0), (0, 0)))[:, :-1]


def wkv7_scan(r, decay, k, v, kk, a):
    B, S, H, N = r.shape

    def step(state, inp):
        r_t, w_t, k_t, v_t, kk_t, a_t = inp
        sa = jnp.einsum('bhvk,bhk->bhv', state, kk_t)
        state = (state * w_t[:, :, None, :]
                 - sa[..., None] * (kk_t * a_t)[:, :, None, :]
                 + v_t[..., None] * k_t[:, :, None, :])
        return state, jnp.einsum('bhvk,bhk->bhv', state, r_t)

    xs = tuple(jnp.swapaxes(t, 0, 1) for t in (r, decay, k, v, kk, a))
    s0 = jnp.zeros((B, H, N, N), jnp.float32)
    _, ys = lax.scan(step, s0, xs)
    return jnp.swapaxes(ys, 0, 1)


def rwkv7_time_mix(p, shift_mix, decay_base, decay_up, iclr_base, iclr_up, gate_up,
                   key_norm_scale, key_iclr_scale, bonus_scale, lnx_w, lnx_b):
    B, S, _ = p.shape
    f32 = jnp.float32
    z = p + (token_shift(p) - p) * shift_mix
    W = RWKV_WIDTH
    r, k, v, zw, za, zg = jnp.split(
        z, [W, 2 * W, 3 * W, 3 * W + DECAY_LORA, 3 * W + DECAY_LORA + ICLR_LORA], axis=-1)
    w_log = -jax.nn.softplus(-(decay_base + jnp.tanh(zw) @ decay_up)) - 0.5
    decay = jnp.exp(-jnp.exp(w_log.astype(f32)))
    a = jax.nn.sigmoid(iclr_base + za @ iclr_up)
    g = jax.nn.sigmoid(zg) @ gate_up

    def heads(t):
        return t.astype(f32).reshape(B, S, RWKV_HEADS, RWKV_HEAD_DIM)

    kk = heads(k * key_norm_scale)
    kk = kk * lax.rsqrt(jnp.maximum(jnp.sum(kk * kk, axis=-1, keepdims=True), 1e-24))
    k = k * (1.0 + (a - 1.0) * key_iclr_scale)
    rh, kh, vh, ah = heads(r), heads(k), heads(v), heads(a)
    y = wkv7_scan(rh, heads(decay), kh, vh, kk, ah)
    mu = jnp.mean(y, axis=-1, keepdims=True)
    var = jnp.mean(jnp.square(y - mu), axis=-1, keepdims=True)
    y = ((y - mu) * lax.rsqrt(var + GROUP_NORM_EPS)).reshape(B, S, W)
    y = y * lnx_w.astype(f32) + lnx_b.astype(f32)
    bonus = jnp.sum(rh * kh * bonus_scale.astype(f32), axis=-1, keepdims=True) * vh
    out = (y + bonus.reshape(B, S, W)) * g.astype(f32)
    return out.astype(p.dtype)


def chunk_attention(q, k, v, rel_bias):
    B, S, _ = q.shape
    n_chunks = S // CHUNK
    pad = LEFT_CHUNKS * CHUNK
    q = q.reshape(B, S, ATT_HEADS, ATT_HEAD_DIM)
    k_pad = jnp.pad(k.reshape(B, S, ATT_HEADS, ATT_HEAD_DIM), ((0, 0), (pad, 0), (0, 0), (0, 0)))
    v_pad = jnp.pad(v.reshape(B, S, ATT_HEADS, ATT_HEAD_DIM), ((0, 0), (pad, 0), (0, 0), (0, 0)))
    dist = jnp.arange(CHUNK)[:, None] - jnp.arange(BAND)[None, :] + pad
    idx = jnp.minimum(dist, REL_CLIP) + (CHUNK - 1)
    bias = rel_bias.astype(jnp.float32)[:, idx]
    scale = ATT_HEAD_DIM ** -0.5

    def one_chunk(c):
        start = c * CHUNK
        qc = lax.dynamic_slice_in_dim(q, start, CHUNK, axis=1)
        kc = lax.dynamic_slice_in_dim(k_pad, start, BAND, axis=1)
        vc = lax.dynamic_slice_in_dim(v_pad, start, BAND, axis=1)
        s = jnp.einsum('bqhd,bkhd->bhqk', qc, kc).astype(jnp.float32) * scale + bias
        valid = (start - pad + jnp.arange(BAND)) >= 0
        s = jnp.where(valid[None, None, None, :], s, MASK_VALUE)
        pr = jax.nn.softmax(s, axis=-1).astype(vc.dtype)
        return jnp.einsum('bhqk,bkhd->bqhd', pr, vc)

    out = lax.map(one_chunk, jnp.arange(n_chunks))
    return jnp.transpose(out, (1, 0, 2, 3, 4)).reshape(B, S, ATT_WIDTH)


def memory_cross_attention(h, mem_n, w_q, w_kv, w_o):
    B, S, _ = h.shape
    M = mem_n.shape[1]
    q = (h @ w_q).reshape(B, S, MEM_HEADS, MEM_HEAD_DIM)
    k, v = jnp.split(mem_n @ w_kv, 2, axis=-1)
    k = k.reshape(B, M, MEM_HEADS, MEM_HEAD_DIM)
    v = v.reshape(B, M, MEM_HEADS, MEM_HEAD_DIM)
    s = jnp.einsum('bshd,bmhd->bhsm', q, k).astype(jnp.float32) * (MEM_HEAD_DIM ** -0.5)
    pr = jax.nn.softmax(s, axis=-1).astype(v.dtype)
    o = jnp.einsum('bhsm,bmhd->bshd', pr, v).reshape(B, S, MEM_WIDTH)
    return o @ w_o


def swiglu_ffn(h, w_in, w_out):
    gate, up = jnp.split(h @ w_in, 2, axis=-1)
    return (jax.nn.silu(gate) * up) @ w_out


def setup_inputs(seed: int = 0) -> dict:
    key = jax.random.key(seed)
    ks = iter(jax.random.split(key, 48))

    def nrm(shape, scale):
        return jax.random.normal(next(ks), shape, jnp.float32) * scale

    def gain(width=D_MODEL):
        return 1.0 + nrm((DEPTH, width), 0.02)

    L = DEPTH
    return {
        "x": nrm((BATCH, SEQ, D_MODEL), 1.0),
        "mem": nrm((BATCH, MEM_TOKENS, D_MODEL), 1.0),
        "g_pre_mix": gain(),
        "g_post_mix": gain(),
        "w_in": nrm((L, D_MODEL, IN_WIDTH), D_MODEL ** -0.5),
        "shift_mix": jax.random.uniform(next(ks), (L, RWKV_IN), jnp.float32),
        "decay_base": jax.random.uniform(next(ks), (L, RWKV_WIDTH), jnp.float32, -6.0, -1.0),
        "decay_up": nrm((L, DECAY_LORA, RWKV_WIDTH), DECAY_LORA ** -0.5),
        "iclr_base": nrm((L, RWKV_WIDTH), 0.5),
        "iclr_up": nrm((L, ICLR_LORA, RWKV_WIDTH), ICLR_LORA ** -0.5),
        "gate_up": nrm((L, GATE_LORA, RWKV_WIDTH), GATE_LORA ** -0.5),
        "key_norm_scale": 0.85 + nrm((L, RWKV_WIDTH), 0.02),
        "key_iclr_scale": 1.0 + nrm((L, RWKV_WIDTH), 0.02),
        "bonus_scale": nrm((L, RWKV_HEADS, RWKV_HEAD_DIM), 0.1),
        "lnx_w": gain(RWKV_WIDTH),
        "lnx_b": nrm((L, RWKV_WIDTH), 0.02),
        "rel_bias": nrm((L, ATT_HEADS, REL_TABLE), 0.1),
        "w_branch_a": nrm((L, RWKV_WIDTH, D_MODEL), RWKV_WIDTH ** -0.5),
        "w_branch_b": nrm((L, ATT_WIDTH, D_MODEL), ATT_WIDTH ** -0.5),
        "w_out": nrm((L, D_MODEL, D_MODEL), D_MODEL ** -0.5),
        "g_pre_cross": gain(),
        "g_post_cross": gain(),
        "g_mem": gain(),
        "w_q_mem": nrm((L, D_MODEL, MEM_WIDTH), D_MODEL ** -0.5),
        "w_kv_mem": nrm((L, D_MODEL, 2 * MEM_WIDTH), D_MODEL ** -0.5),
        "w_o_mem": nrm((L, MEM_WIDTH, D_MODEL), MEM_WIDTH ** -0.5),
        "g_pre_ffn": gain(),
        "g_post_ffn": gain(),
        "w_ffn_in": nrm((L, D_MODEL, 2 * FFN_HIDDEN), D_MODEL ** -0.5),
        "w_ffn_out": nrm((L, FFN_HIDDEN, D_MODEL), FFN_HIDDEN ** -0.5),
    }


def reference(x, mem, g_pre_mix, g_post_mix, w_in, shift_mix, decay_base, decay_up,
              iclr_base, iclr_up, gate_up, key_norm_scale, key_iclr_scale, bonus_scale,
              lnx_w, lnx_b, rel_bias, w_branch_a, w_branch_b, w_out,
              g_pre_cross, g_post_cross, g_mem, w_q_mem, w_kv_mem, w_o_mem,
              g_pre_ffn, g_post_ffn, w_ffn_in, w_ffn_out):
    split_at = [RWKV_IN,
                RWKV_IN + ATT_WIDTH,
                RWKV_IN + 2 * ATT_WIDTH,
                RWKV_IN + 3 * ATT_WIDTH,
                RWKV_IN + 3 * ATT_WIDTH + D_MODEL]
    for l in range(DEPTH):
        h = rms_norm(x, g_pre_mix[l])
        proj = h @ w_in[l]
        p_rwkv, q, k, v, z_ga, z_gb = jnp.split(proj, split_at, axis=-1)
        y_a = rwkv7_time_mix(p_rwkv, shift_mix[l], decay_base[l], decay_up[l],
                             iclr_base[l], iclr_up[l], gate_up[l], key_norm_scale[l],
                             key_iclr_scale[l], bonus_scale[l], lnx_w[l], lnx_b[l])
        y_b = chunk_attention(q, k, v, rel_bias[l])
        mixed = (jax.nn.sigmoid(z_ga) * (y_a @ w_branch_a[l])
                 + jax.nn.sigmoid(z_gb) * (y_b @ w_branch_b[l]))
        x = x + rms_norm(mixed @ w_out[l], g_post_mix[l])
        h = rms_norm(x, g_pre_cross[l])
        m = rms_norm(mem, g_mem[l])
        x = x + rms_norm(memory_cross_attention(h, m, w_q_mem[l], w_kv_mem[l], w_o_mem[l]),
                         g_post_cross[l])
        h = rms_norm(x, g_pre_ffn[l])
        x = x + rms_norm(swiglu_ffn(h, w_ffn_in[l], w_ffn_out[l]), g_post_ffn[l])
    return x
```

```python
import functools

import jax
import jax.numpy as jnp
from jax import lax
from jax.experimental import pallas as pl
from jax.experimental.pallas import tpu as pltpu

F32 = jnp.float32
BF16 = jnp.bfloat16

NORM_EPS = 1e-6
GROUP_NORM_EPS = 64e-5
MASK_VALUE = -1e30

HEAD_DIM = 64
LANE_GROUP = 256
HEADS_PER_GROUP = LANE_GROUP // HEAD_DIM
CHUNK = 64
LEFT_CHUNKS = 8
REL_CLIP = 128
Q_BLOCK = 2 * CHUNK
KEY_WINDOW = LEFT_CHUNKS * CHUNK + Q_BLOCK
LORA_SLOT = 128
VMEM_LIMIT = 48 * 1024 * 1024


def _dot(a, b):
    return jnp.dot(a, b, preferred_element_type=F32)


def _dot_nt(a, b):
    return lax.dot_general(a, b, (((1,), (1,)), ((), ())), preferred_element_type=F32)


def _dot_tn(a, b):
    return lax.dot_general(a, b, (((0,), (0,)), ((), ())), preferred_element_type=F32)


def _sigmoid(u):
    return 1.0 / (1.0 + jnp.exp(-u))


def _rms_norm(xf, gain):
    ms = jnp.mean(xf * xf, axis=-1, keepdims=True)
    return xf * lax.rsqrt(ms + NORM_EPS) * gain


def _params(*sem):
    return pltpu.CompilerParams(dimension_semantics=sem, vmem_limit_bytes=VMEM_LIMIT)


def _norm_proj_kernel(x_ref, g_ref, w_ref, mix_ref, o_ref, h_scr, carry_scr, *, n_shift, seq_tiles):
    m = pl.program_id(0)
    n = pl.program_id(1)

    @pl.when(n == 0)
    def _():
        h_scr[...] = _rms_norm(x_ref[...], g_ref[...]).astype(BF16)

    p = _dot(h_scr[...], w_ref[...])
    bm = p.shape[0]

    def plain():
        o_ref[...] = p.astype(o_ref.dtype)

    def shifted():
        @pl.when(m % seq_tiles == 0)
        def _():
            carry_scr[n] = jnp.zeros(carry_scr.shape[1:], F32)

        row = lax.broadcasted_iota(jnp.int32, p.shape, 0)
        prev = jnp.where(row == 0, carry_scr[n][7:8, :], pltpu.roll(p, 1, 0))
        o_ref[...] = (p + (prev - p) * mix_ref[...]).astype(o_ref.dtype)
        carry_scr[n] = p[bm - 8:, :]

    if n_shift == 0:
        plain()
    else:
        pl.when(n < n_shift)(shifted)
        pl.when(n >= n_shift)(plain)


def _norm_proj(x2d, gain, w, mix, *, n_shift, seq_len, bm, bn):
    t, d = x2d.shape
    n_out = w.shape[1]
    kern = functools.partial(_norm_proj_kernel, n_shift=n_shift, seq_tiles=seq_len // bm)
    return pl.pallas_call(
        kern,
        grid=(t // bm, n_out // bn),
        in_specs=[
            pl.BlockSpec((bm, d), lambda m, n: (m, 0)),
            pl.BlockSpec((1, d), lambda m, n: (0, 0)),
            pl.BlockSpec((d, bn), lambda m, n: (0, n)),
            pl.BlockSpec((1, bn), lambda m, n: (0, n)),
        ],
        out_specs=pl.BlockSpec((bm, bn), lambda m, n: (m, n)),
        out_shape=jax.ShapeDtypeStruct((t, n_out), BF16),
        scratch_shapes=[
            pltpu.VMEM((bm, d), BF16),
            pltpu.VMEM((max(n_shift, 1), 8, bn), F32),
        ],
        compiler_params=_params("arbitrary", "arbitrary"),
        name="norm_proj",
    )(x2d, gain, w, mix)


def _rwkv_kernel(r_ref, k_ref, v_ref, zl_ref, wdec_ref, wicl_ref, wgate_ref, vec_ref,
                 o_ref, st_scr, *, n_chunks):
    C, G, H = CHUNK, LANE_GROUP, HEADS_PER_GROUP

    tt = lax.broadcasted_iota(jnp.int32, (C, G), 0)
    ss = lax.broadcasted_iota(jnp.int32, (C, G), 1) & (C - 1)
    m_strict = ss < tt
    m_incl = ss <= tt
    eye_c = jnp.where(ss == tt, 1.0, 0.0).astype(F32)
    r0 = lax.broadcasted_iota(jnp.int32, (G, G), 0)
    c0 = lax.broadcasted_iota(jnp.int32, (G, G), 1)
    blk = (r0 >> 6) == (c0 >> 6)
    ones_bd = jnp.where(blk, 1.0, 0.0).astype(BF16)
    eye_g = jnp.where(r0 == c0, 1.0, 0.0).astype(F32)
    tr = lax.broadcasted_iota(jnp.int32, (C, C), 0)
    tc = lax.broadcasted_iota(jnp.int32, (C, C), 1)
    tri = jnp.where(tc <= tr, 1.0, 0.0).astype(BF16)

    vec = vec_ref[...]
    decay_base, iclr_base = vec[0:1], vec[1:2]
    kns, kis, bonus_scale = vec[2:3], vec[3:4], vec[4:5]
    lnx_w, lnx_b = vec[5:6], vec[6:7]

    def bd(xc):
        return jnp.where(blk, jnp.concatenate([xc] * H, axis=0), 0.0).astype(BF16)

    def seg_sum(xc):
        return _dot(xc.astype(BF16), ones_bd)

    st_scr[...] = jnp.zeros(st_scr.shape, F32)

    def chunk(j, carry):
        t0 = pl.multiple_of(j * C, C)
        rc = r_ref[pl.ds(t0, C), :].astype(F32)
        kc = k_ref[pl.ds(t0, C), :].astype(F32)
        vc = v_ref[pl.ds(t0, C), :].astype(F32)
        zl = zl_ref[pl.ds(t0, C), :]
        zw = zl[:, 0:LORA_SLOT].astype(F32)
        za = zl[:, LORA_SLOT:2 * LORA_SLOT]
        zg = zl[:, 2 * LORA_SLOT:].astype(F32)

        u = decay_base + _dot(jnp.tanh(zw).astype(BF16), wdec_ref[...])
        softplus_neg_u = jnp.maximum(-u, 0.0) + jnp.log(1.0 + jnp.exp(-jnp.abs(u)))
        ld = -jnp.exp(-softplus_neg_u - 0.5)
        a = _sigmoid(iclr_base + _dot(za, wicl_ref[...]))
        gate = _dot(_sigmoid(zg).astype(BF16), wgate_ref[...])

        kk = kc * kns
        kk = kk * lax.rsqrt(jnp.maximum(seg_sum(kk * kk), 1e-24))
        k2 = kc * (1.0 + (a - 1.0) * kis)
        beta = kk * a
        bonus = seg_sum(rc * k2 * bonus_scale) * vc

        ld_hi = ld.astype(BF16)
        ld_lo = (ld - ld_hi.astype(F32)).astype(BF16)
        cum = _dot(tri, ld_hi) + _dot(tri, ld_lo)
        cum_end = cum[C - 1:C, :]
        p_t = jnp.exp(cum)
        p_inv = jnp.exp(-cum)
        p_prev = jnp.exp(cum - ld)
        p_end = jnp.exp(cum_end - cum)
        rt, kt, bt, kkt = rc * p_t, k2 * p_inv, beta * p_inv, kk * p_prev
        kh, bh = k2 * p_end, beta * p_end

        lhs = jnp.concatenate([kkt, rt], axis=0).astype(BF16)
        ab = _dot_nt(lhs, bd(bt))
        ak = _dot_nt(lhs, bd(kt))
        a1 = jnp.where(m_strict, ab[:C], 0.0)
        a4 = jnp.where(m_incl, ab[C:], 0.0)
        a2 = jnp.where(m_strict, ak[:C], 0.0)
        a3 = jnp.where(m_incl, ak[C:], 0.0)

        x = -a1
        t_inv = eye_c + x
        xn = _dot(x.astype(BF16), bd(x))
        for level in range(1, 6):
            xbd = bd(xn)
            if level < 5:
                both = _dot(jnp.concatenate([t_inv, xn], axis=0).astype(BF16), xbd)
                t_inv = t_inv + both[:C]
                xn = both[C:]
            else:
                t_inv = t_inv + _dot(t_inv.astype(BF16), xbd)

        av = _dot(jnp.concatenate([a2, a3], axis=0).astype(BF16), bd(vc))
        a2v, a3v = av[:C], av[C:]
        t16 = t_inv.astype(BF16)
        g_mat = _dot(t16, bd(kkt))
        u0 = _dot(t16, bd(a2v))
        a4_16 = a4.astype(BF16)
        q_mat = rt - _dot(a4_16, bd(g_mat))
        y0 = a3v - _dot(a4_16, bd(u0))

        st16 = st_scr[...].astype(BF16)
        y = _dot(q_mat.astype(BF16), st16) + y0

        bh16 = bh.astype(BF16)
        phi_t = eye_g * p_t[C - 1:C, :] - jnp.where(blk, _dot_tn(bh16, g_mat.astype(BF16)), 0.0)
        z_t = jnp.where(
            blk,
            _dot_tn(kh.astype(BF16), vc.astype(BF16)) - _dot_tn(bh16, u0.astype(BF16)),
            0.0)
        st_scr[...] = _dot(phi_t.astype(BF16), st16) + z_t

        mu = seg_sum(y) * (1.0 / HEAD_DIM)
        d = y - mu
        var = seg_sum(d * d) * (1.0 / HEAD_DIM)
        yn = d * lax.rsqrt(var + GROUP_NORM_EPS) * lnx_w + lnx_b
        o_ref[pl.ds(t0, C), :] = ((yn + bonus) * gate).astype(o_ref.dtype)
        return carry

    lax.fori_loop(0, n_chunks, chunk, 0)


def _rwkv(proj, zl, wdec, wicl, wgate, vec, *, batch, seq_len, r_col, k_col, v_col):
    G = LANE_GROUP
    n_groups = wdec.shape[1] // G
    kern = functools.partial(_rwkv_kernel, n_chunks=seq_len // CHUNK)

    def col(base):
        return pl.BlockSpec((seq_len, G), lambda b, g: (b, base // G + g))

    return pl.pallas_call(
        kern,
        grid=(batch, n_groups),
        in_specs=[
            col(r_col), col(k_col), col(v_col),
            pl.BlockSpec((seq_len, zl.shape[1]), lambda b, g: (b, 0)),
            pl.BlockSpec((wdec.shape[0], G), lambda b, g: (0, g)),
            pl.BlockSpec((wicl.shape[0], G), lambda b, g: (0, g)),
            pl.BlockSpec((wgate.shape[0], G), lambda b, g: (0, g)),
            pl.BlockSpec((8, G), lambda b, g: (0, g)),
        ],
        out_specs=pl.BlockSpec((seq_len, G), lambda b, g: (b, g)),
        out_shape=jax.ShapeDtypeStruct((batch * seq_len, n_groups * G), BF16),
        scratch_shapes=[pltpu.VMEM((G, G), F32)],
        compiler_params=_params("arbitrary", "arbitrary"),
        name="rwkv_time_mix",
    )(proj, proj, proj, zl, wdec, wicl, wgate, vec)


def _band_attn_kernel(q_ref, k_ref, v_ref, bias_ref, o_ref, kpad, vpad, *, seq_len):
    G, H, QB, KW = LANE_GROUP, HEADS_PER_GROUP, Q_BLOCK, KEY_WINDOW
    pad = KW - QB
    kpad[0:pad, :] = jnp.zeros((pad, G), BF16)
    vpad[0:pad, :] = jnp.zeros((pad, G), BF16)
    kpad[pad:, :] = k_ref[...]
    vpad[pad:, :] = v_ref[...]

    lane_head = lax.broadcasted_iota(jnp.int32, (QB, G), 1) >> 6
    key_idx = lax.broadcasted_iota(jnp.int32, (H * QB, KW), 1)
    scale = HEAD_DIM ** -0.5

    def block(qb, carry):
        q0 = pl.multiple_of(qb * QB, QB)
        qv = q_ref[pl.ds(q0, QB), :].astype(F32)
        kw = kpad[pl.ds(q0, KW), :]
        vw = vpad[pl.ds(q0, KW), :]
        qs = jnp.concatenate(
            [jnp.where(lane_head == h, qv, 0.0) for h in range(H)], axis=0).astype(BF16)
        s = _dot_nt(qs, kw) * scale + bias_ref[...].reshape(H * QB, KW)
        s = jnp.where(key_idx + (q0 - pad) >= 0, s, MASK_VALUE)
        m = jnp.max(s, axis=-1, keepdims=True)
        p = jnp.exp(s - m)
        denom = jnp.sum(p, axis=-1, keepdims=True)
        o = _dot(p.astype(BF16), vw) / denom
        out = jnp.where(lane_head == 0, o[0:QB], 0.0)
        for h in range(1, H):
            out = out + jnp.where(lane_head == h, o[h * QB:(h + 1) * QB], 0.0)
        o_ref[pl.ds(q0, QB), :] = out.astype(o_ref.dtype)
        return carry

    lax.fori_loop(0, seq_len // QB, block, 0)


def _band_attn(proj, bias, *, batch, seq_len, q_col, k_col, v_col, width):
    G, H = LANE_GROUP, HEADS_PER_GROUP
    n_groups = width // G
    kern = functools.partial(_band_attn_kernel, seq_len=seq_len)

    def col(base):
        return pl.BlockSpec((seq_len, G), lambda b, g: (b, base // G + g))

    return pl.pallas_call(
        kern,
        grid=(batch, n_groups),
        in_specs=[
            col(q_col), col(k_col), col(v_col),
            pl.BlockSpec((H, Q_BLOCK, KEY_WINDOW), lambda b, g: (g, 0, 0)),
        ],
        out_specs=pl.BlockSpec((seq_len, G), lambda b, g: (b, g)),
        out_shape=jax.ShapeDtypeStruct((batch * seq_len, width), BF16),
        scratch_shapes=[
            pltpu.VMEM((seq_len + KEY_WINDOW - Q_BLOCK, G), BF16),
            pltpu.VMEM((seq_len + KEY_WINDOW - Q_BLOCK, G), BF16),
        ],
        compiler_params=_params("arbitrary", "arbitrary"),
        name="band_attention",
    )(proj, proj, proj, bias)


def _band_bias(rel_bias):
    i = jnp.arange(Q_BLOCK)[:, None]
    j = jnp.arange(KEY_WINDOW)[None, :]
    dist = i - j + (KEY_WINDOW - Q_BLOCK)
    idx = jnp.clip(jnp.minimum(dist, REL_CLIP) + (CHUNK - 1), 0, rel_bias.shape[1] - 1)
    qc, kc = i // CHUNK, j // CHUNK
    band = (kc >= qc) & (kc <= qc + LEFT_CHUNKS)
    return jnp.where(band[None], rel_bias.astype(F32)[:, idx], MASK_VALUE)


def _merge_kernel(x_ref, ya_ref, yb_ref, za_ref, zb_ref, wa_ref, wb_ref, wo_ref, g_ref, o_ref):
    mixed = (_sigmoid(za_ref[...].astype(F32)) * _dot(ya_ref[...], wa_ref[...])
             + _sigmoid(zb_ref[...].astype(F32)) * _dot(yb_ref[...], wb_ref[...]))
    y = _dot(mixed.astype(BF16), wo_ref[...])
    o_ref[...] = x_ref[...] + _rms_norm(y, g_ref[...])


def _merge(x2d, ya, yb, proj, wa, wb, wo, gain, *, gate_col, bm):
    t, d = x2d.shape
    row = lambda c: pl.BlockSpec((bm, d), lambda m: (m, c))
    full = lambda a: pl.BlockSpec(a.shape, lambda m: (0, 0))
    return pl.pallas_call(
        _merge_kernel,
        grid=(t // bm,),
        in_specs=[row(0), row(0), row(0), row(gate_col // d), row(gate_col // d + 1),
                  full(wa), full(wb), full(wo), full(gain)],
        out_specs=row(0),
        out_shape=jax.ShapeDtypeStruct((t, d), F32),
        compiler_params=_params("arbitrary"),
        name="gated_merge",
    )(x2d, ya, yb, proj, proj, wa, wb, wo, gain)


def _cross_attn_kernel(x_ref, kv_ref, wq_ref, wo_ref, gpre_ref, gpost_ref, o_ref, *, n_heads):
    x = x_ref[...]
    h = _rms_norm(x, gpre_ref[...]).astype(BF16)
    q = _dot(h, wq_ref[...]).astype(BF16)
    width = q.shape[1]
    hd = width // n_heads
    scale = hd ** -0.5
    outs = []
    for i in range(n_heads):
        qh = q[:, i * hd:(i + 1) * hd]
        kh = kv_ref[:, i * hd:(i + 1) * hd]
        vh = kv_ref[:, width + i * hd:width + (i + 1) * hd]
        s = _dot_nt(qh, kh) * scale
        m = jnp.max(s, axis=-1, keepdims=True)
        p = jnp.exp(s - m)
        denom = jnp.sum(p, axis=-1, keepdims=True)
        outs.append((_dot(p.astype(BF16), vh) / denom).astype(BF16))
    o = jnp.concatenate(outs, axis=-1)
    y = _dot(o, wo_ref[...])
    o_ref[...] = x + _rms_norm(y, gpost_ref[...])


def _cross_attn(x2d, kv, wq, wo, gpre, gpost, *, seq_len, mem_tokens, n_heads, bm):
    t, d = x2d.shape
    tiles_per_seq = seq_len // bm
    full = lambda a: pl.BlockSpec(a.shape, lambda m: (0, 0))
    kern = functools.partial(_cross_attn_kernel, n_heads=n_heads)
    return pl.pallas_call(
        kern,
        grid=(t // bm,),
        in_specs=[
            pl.BlockSpec((bm, d), lambda m: (m, 0)),
            pl.BlockSpec((mem_tokens, kv.shape[1]), lambda m: (m // tiles_per_seq, 0)),
            full(wq), full(wo), full(gpre), full(gpost),
        ],
        out_specs=pl.BlockSpec((bm, d), lambda m: (m, 0)),
        out_shape=jax.ShapeDtypeStruct((t, d), F32),
        compiler_params=_params("arbitrary"),
        name="cross_attention",
    )(x2d, kv, wq, wo, gpre, gpost)


def _ffn_kernel(x_ref, wg_ref, wu_ref, wo_ref, gpre_ref, gpost_ref, o_ref):
    x = x_ref[...]
    h = _rms_norm(x, gpre_ref[...]).astype(BF16)
    acc = jnp.zeros(x.shape, F32)
    for j in range(wg_ref.shape[0]):
        gate = _dot(h, wg_ref[j])
        up = _dot(h, wu_ref[j])
        act = (gate * _sigmoid(gate) * up).astype(BF16)
        acc = acc + _dot(act, wo_ref[j])
    o_ref[...] = x + _rms_norm(acc, gpost_ref[...])


def _ffn(x2d, wg, wu, wo, gpre, gpost, *, bm):
    t, d = x2d.shape
    resident = lambda a: pl.BlockSpec(a.shape, lambda m: (0,) * a.ndim,
                                      pipeline_mode=pl.Buffered(1))
    return pl.pallas_call(
        _ffn_kernel,
        grid=(t // bm,),
        in_specs=[
            pl.BlockSpec((bm, d), lambda m: (m, 0)),
            resident(wg), resident(wu), resident(wo), resident(gpre), resident(gpost),
        ],
        out_specs=pl.BlockSpec((bm, d), lambda m: (m, 0)),
        out_shape=jax.ShapeDtypeStruct((t, d), F32),
        compiler_params=_params("arbitrary"),
        name="swiglu_ffn",
    )(x2d, wg, wu, wo, gpre, gpost)


def _pad_to(a, rows=None, cols=None):
    r = a.shape[0] if rows is None else rows
    c = a.shape[1] if cols is None else cols
    return jnp.pad(a, ((0, r - a.shape[0]), (0, c - a.shape[1])))


def _layer(x2d, mem2d, p, *, batch, seq_len):
    d = x2d.shape[1]
    width = p["decay_up"].shape[1]
    n_dec, n_icl, n_gate = p["decay_up"].shape[0], p["iclr_up"].shape[0], p["gate_up"].shape[0]
    gate_slot = -(-n_gate // LORA_SLOT) * LORA_SLOT
    rwkv_in = 3 * width + n_dec + n_icl + n_gate
    row = lambda v: v.reshape(1, -1).astype(F32)

    w_in, mix = p["w_in"], p["shift_mix"]
    lo = 3 * width
    main_cols = jnp.concatenate([w_in[:, :lo], w_in[:, rwkv_in:]], axis=1)
    main_mix = jnp.concatenate([mix[:lo], jnp.zeros((main_cols.shape[1] - lo,), F32)])
    lora_w = jnp.concatenate([
        _pad_to(w_in[:, lo:lo + n_dec], cols=LORA_SLOT),
        _pad_to(w_in[:, lo + n_dec:lo + n_dec + n_icl], cols=LORA_SLOT),
        _pad_to(w_in[:, lo + n_dec + n_icl:rwkv_in], cols=gate_slot)], axis=1)
    lora_mix = jnp.concatenate([
        jnp.pad(mix[lo:lo + n_dec], (0, LORA_SLOT - n_dec)),
        jnp.pad(mix[lo + n_dec:lo + n_dec + n_icl], (0, LORA_SLOT - n_icl)),
        jnp.pad(mix[lo + n_dec + n_icl:rwkv_in], (0, gate_slot - n_gate))])
    wdec = _pad_to(p["decay_up"], rows=LORA_SLOT).astype(BF16)
    wicl = _pad_to(p["iclr_up"], rows=LORA_SLOT).astype(BF16)
    wgate = _pad_to(p["gate_up"], rows=gate_slot).astype(BF16)
    vec = jnp.stack([p["decay_base"], p["iclr_base"], p["key_norm_scale"], p["key_iclr_scale"],
                     p["bonus_scale"].reshape(-1), p["lnx_w"], p["lnx_b"],
                     jnp.zeros((width,), F32)]).astype(F32)

    bm_proj = min(1024, seq_len)
    proj = _norm_proj(x2d, row(p["g_pre_mix"]), main_cols.astype(BF16), row(main_mix),
                      n_shift=lo // 1024, seq_len=seq_len, bm=bm_proj, bn=1024)
    zl = _norm_proj(x2d, row(p["g_pre_mix"]), lora_w.astype(BF16), row(lora_mix),
                    n_shift=1, seq_len=seq_len, bm=bm_proj, bn=lora_w.shape[1])
    ya = _rwkv(proj, zl, wdec, wicl, wgate, vec, batch=batch, seq_len=seq_len,
               r_col=0, k_col=width, v_col=2 * width)
    yb = _band_attn(proj, _band_bias(p["rel_bias"]), batch=batch, seq_len=seq_len,
                    q_col=3 * width, k_col=4 * width, v_col=5 * width, width=width)
    x2d = _merge(x2d, ya, yb, proj, p["w_branch_a"].astype(BF16), p["w_branch_b"].astype(BF16),
                 p["w_out"].astype(BF16), row(p["g_post_mix"]), gate_col=6 * width, bm=512)

    mem_tokens = mem2d.shape[0] // batch
    zero_mix = jnp.zeros((1, p["w_kv_mem"].shape[1]), F32)
    kv = _norm_proj(mem2d, row(p["g_mem"]), p["w_kv_mem"].astype(BF16), zero_mix,
                    n_shift=0, seq_len=mem_tokens, bm=min(1024, mem2d.shape[0]),
                    bn=p["w_kv_mem"].shape[1])
    x2d = _cross_attn(x2d, kv, p["w_q_mem"].astype(BF16), p["w_o_mem"].astype(BF16),
                      row(p["g_pre_cross"]), row(p["g_post_cross"]),
                      seq_len=seq_len, mem_tokens=mem_tokens, n_heads=4, bm=512)

    hidden = p["w_ffn_out"].shape[0]
    n_hc = hidden // LANE_GROUP
    w_ffn_in = p["w_ffn_in"].astype(BF16)
    wg = w_ffn_in[:, :hidden].reshape(d, n_hc, LANE_GROUP).transpose(1, 0, 2)
    wu = w_ffn_in[:, hidden:].reshape(d, n_hc, LANE_GROUP).transpose(1, 0, 2)
    wo = p["w_ffn_out"].astype(BF16).reshape(n_hc, LANE_GROUP, d)
    return _ffn(x2d, wg, wu, wo, row(p["g_pre_ffn"]), row(p["g_post_ffn"]), bm=512)


def kernel(x, mem, g_pre_mix, g_post_mix, w_in, shift_mix, decay_base, decay_up, iclr_base, iclr_up, gate_up, key_norm_scale, key_iclr_scale, bonus_scale, lnx_w, lnx_b, rel_bias, w_branch_a, w_branch_b, w_out, g_pre_cross, g_post_cross, g_mem, w_q_mem, w_kv_mem, w_o_mem, g_pre_ffn, g_post_ffn, w_ffn_in, w_ffn_out):
    batch, seq_len, d = x.shape
    stacked = dict(
        g_pre_mix=g_pre_mix, g_post_mix=g_post_mix, w_in=w_in, shift_mix=shift_mix,
        decay_base=decay_base, decay_up=decay_up, iclr_base=iclr_base, iclr_up=iclr_up,
        gate_up=gate_up, key_norm_scale=key_norm_scale, key_iclr_scale=key_iclr_scale,
        bonus_scale=bonus_scale, lnx_w=lnx_w, lnx_b=lnx_b, rel_bias=rel_bias,
        w_branch_a=w_branch_a, w_branch_b=w_branch_b, w_out=w_out, g_pre_cross=g_pre_cross,
        g_post_cross=g_post_cross, g_mem=g_mem, w_q_mem=w_q_mem, w_kv_mem=w_kv_mem,
        w_o_mem=w_o_mem, g_pre_ffn=g_pre_ffn, g_post_ffn=g_post_ffn, w_ffn_in=w_ffn_in,
        w_ffn_out=w_ffn_out)
    x2d = x.reshape(batch * seq_len, d)
    mem2d = mem.reshape(-1, d)
    for layer in range(g_pre_mix.shape[0]):
        x2d = _layer(x2d, mem2d, {k: v[layer] for k, v in stacked.items()},
                     batch=batch, seq_len=seq_len)
    return x2d.reshape(batch, seq_len, d)
```

```python
import functools
import itertools

import jax
import jax.numpy as jnp
from jax import lax
from jax.experimental import pallas as pl
from jax.experimental.pallas import tpu as pltpu

F32 = jnp.float32
BF16 = jnp.bfloat16

NORM_EPS = 1e-6
GROUP_NORM_EPS = 64e-5
MASK_VALUE = -1e30

HEAD_DIM = 64
LANE_GROUP = 256
HEADS_PER_GROUP = LANE_GROUP // HEAD_DIM
CHUNK = 64
LEFT_CHUNKS = 8
REL_CLIP = 128
Q_BLOCK = 2 * CHUNK
KEY_WINDOW = LEFT_CHUNKS * CHUNK + Q_BLOCK
LORA_SLOT = 128
VMEM_LIMIT = 48 * 1024 * 1024


def _dot(a, b):
    return jnp.dot(a, b, preferred_element_type=F32)


def _dot_nt(a, b):
    return lax.dot_general(a, b, (((1,), (1,)), ((), ())), preferred_element_type=F32)


def _dot_tn(a, b):
    return lax.dot_general(a, b, (((0,), (0,)), ((), ())), preferred_element_type=F32)


def _sigmoid(u):
    return 1.0 / (1.0 + jnp.exp(-u))


def _rms_norm(xf, gain):
    ms = jnp.mean(xf * xf, axis=-1, keepdims=True)
    return xf * lax.rsqrt(ms + NORM_EPS) * gain


def _params(*sem):
    return pltpu.CompilerParams(dimension_semantics=sem, vmem_limit_bytes=VMEM_LIMIT)


def _norm_proj_kernel(x_ref, g_ref, w_ref, mix_ref, o_ref, h_scr, carry_scr, *, n_shift, seq_tiles):
    m = pl.program_id(0)
    n = pl.program_id(1)

    @pl.when(n == 0)
    def _():
        h_scr[...] = _rms_norm(x_ref[...], g_ref[...]).astype(BF16)

    p = _dot(h_scr[...], w_ref[...])
    bm = p.shape[0]

    def plain():
        o_ref[...] = p.astype(o_ref.dtype)

    def shifted():
        @pl.when(m % seq_tiles == 0)
        def _():
            carry_scr[n] = jnp.zeros(carry_scr.shape[1:], F32)

        row = lax.broadcasted_iota(jnp.int32, p.shape, 0)
        prev = jnp.where(row == 0, carry_scr[n][7:8, :], pltpu.roll(p, 1, 0))
        o_ref[...] = (p + (prev - p) * mix_ref[...]).astype(o_ref.dtype)
        carry_scr[n] = p[bm - 8:, :]

    if n_shift == 0:
        plain()
    else:
        pl.when(n < n_shift)(shifted)
        pl.when(n >= n_shift)(plain)


def _norm_proj(x2d, gain, w, mix, *, n_shift, seq_len, bm, bn):
    t, d = x2d.shape
    n_out = w.shape[1]
    kern = functools.partial(_norm_proj_kernel, n_shift=n_shift, seq_tiles=seq_len // bm)
    return pl.pallas_call(
        kern,
        grid=(t // bm, n_out // bn),
        in_specs=[
            pl.BlockSpec((bm, d), lambda m, n: (m, 0)),
            pl.BlockSpec((1, d), lambda m, n: (0, 0)),
            pl.BlockSpec((d, bn), lambda m, n: (0, n)),
            pl.BlockSpec((1, bn), lambda m, n: (0, n)),
        ],
        out_specs=pl.BlockSpec((bm, bn), lambda m, n: (m, n)),
        out_shape=jax.ShapeDtypeStruct((t, n_out), BF16),
        scratch_shapes=[
            pltpu.VMEM((bm, d), BF16),
            pltpu.VMEM((max(n_shift, 1), 8, bn), F32),
        ],
        compiler_params=_params("arbitrary", "arbitrary"),
        name="norm_proj",
    )(x2d, gain, w, mix)


def _rwkv_kernel(r_ref, k_ref, v_ref, zl_ref, wdec_ref, wicl_ref, wgate_ref, vec_ref,
                 o_ref, st_scr, *, n_chunks, n_groups):
    C, G, H = CHUNK, LANE_GROUP, HEADS_PER_GROUP

    tt = lax.broadcasted_iota(jnp.int32, (C, G), 0)
    ss = lax.broadcasted_iota(jnp.int32, (C, G), 1) & (C - 1)
    m_strict = ss < tt
    m_incl = ss <= tt
    eye_c = jnp.where(ss == tt, 1.0, 0.0).astype(F32)
    r0 = lax.broadcasted_iota(jnp.int32, (G, G), 0)
    c0 = lax.broadcasted_iota(jnp.int32, (G, G), 1)
    blk = (r0 >> 6) == (c0 >> 6)
    ones_bd = jnp.where(blk, 1.0, 0.0).astype(BF16)
    eye_g = jnp.where(r0 == c0, 1.0, 0.0).astype(F32)
    tr = lax.broadcasted_iota(jnp.int32, (C, C), 0)
    tc = lax.broadcasted_iota(jnp.int32, (C, C), 1)
    tri = jnp.where(tc <= tr, 1.0, 0.0).astype(BF16)

    def bd(xc):
        return jnp.where(blk, jnp.concatenate([xc] * H, axis=0), 0.0).astype(BF16)

    def seg_sum(xc):
        return _dot(xc.astype(BF16), ones_bd)

    @pl.when(pl.program_id(1) == 0)
    def _():
        st_scr[...] = jnp.zeros(st_scr.shape, F32)

    def group_chunk(g, t0, u, a, gate):
        lanes = slice(g * G, (g + 1) * G)
        vec = vec_ref[:, lanes]
        kns, kis, bonus_scale = vec[2:3], vec[3:4], vec[4:5]
        lnx_w, lnx_b = vec[5:6], vec[6:7]
        rc = r_ref[pl.ds(t0, C), lanes].astype(F32)
        kc = k_ref[pl.ds(t0, C), lanes].astype(F32)
        vc = v_ref[pl.ds(t0, C), lanes].astype(F32)

        softplus_neg_u = jnp.maximum(-u, 0.0) + jnp.log(1.0 + jnp.exp(-jnp.abs(u)))
        ld = -jnp.exp(-softplus_neg_u - 0.5)
        ld_hi = ld.astype(BF16)
        ld_lo = (ld - ld_hi.astype(F32)).astype(BF16)
        cum = _dot(tri, ld_hi) + _dot(tri, ld_lo)
        kk = kc * kns
        kk_ss = seg_sum(kk * kk)
        k2 = kc * (1.0 + (a - 1.0) * kis)
        bonus_s = seg_sum(rc * k2 * bonus_scale)
        yield

        kk = kk * lax.rsqrt(jnp.maximum(kk_ss, 1e-24))
        beta = kk * a
        bonus = bonus_s * vc
        cum_end = cum[C - 1:C, :]
        p_t = jnp.exp(cum)
        p_inv = jnp.exp(-cum)
        p_prev = jnp.exp(cum - ld)
        p_end = jnp.exp(cum_end - cum)
        rt, kt, bt, kkt = rc * p_t, k2 * p_inv, beta * p_inv, kk * p_prev
        kh, bh = k2 * p_end, beta * p_end

        lhs = jnp.concatenate([kkt, rt], axis=0).astype(BF16)
        ab = _dot_nt(lhs, bd(bt))
        ak = _dot_nt(lhs, bd(kt))
        yield
        a1 = jnp.where(m_strict, ab[:C], 0.0)
        a4 = jnp.where(m_incl, ab[C:], 0.0)
        a2 = jnp.where(m_strict, ak[:C], 0.0)
        a3 = jnp.where(m_incl, ak[C:], 0.0)

        x = -a1
        t_inv = eye_c + x
        xn = _dot(x.astype(BF16), bd(x))
        av = _dot(jnp.concatenate([a2, a3], axis=0).astype(BF16), bd(vc))
        yield
        a2v, a3v = av[:C], av[C:]
        for level in range(1, 6):
            xbd = bd(xn)
            if level < 5:
                both = _dot(jnp.concatenate([t_inv, xn], axis=0).astype(BF16), xbd)
                yield
                t_inv = t_inv + both[:C]
                xn = both[C:]
            else:
                t_inv = t_inv + _dot(t_inv.astype(BF16), xbd)
                yield

        t16 = t_inv.astype(BF16)
        g_mat = _dot(t16, bd(kkt))
        u0 = _dot(t16, bd(a2v))
        yield
        a4_16 = a4.astype(BF16)
        q_mat = rt - _dot(a4_16, bd(g_mat))
        y0 = a3v - _dot(a4_16, bd(u0))
        bh16 = bh.astype(BF16)
        phi_t = eye_g * p_t[C - 1:C, :] - jnp.where(blk, _dot_tn(bh16, g_mat.astype(BF16)), 0.0)
        z_t = jnp.where(
            blk,
            _dot_tn(kh.astype(BF16), vc.astype(BF16)) - _dot_tn(bh16, u0.astype(BF16)),
            0.0)
        yield

        st16 = st_scr[g].astype(BF16)
        y = _dot(q_mat.astype(BF16), st16) + y0
        st_scr[g] = _dot(phi_t.astype(BF16), st16) + z_t
        yield

        mu = seg_sum(y) * (1.0 / HEAD_DIM)
        yield
        d = y - mu
        var = seg_sum(d * d) * (1.0 / HEAD_DIM)
        yield
        yn = d * lax.rsqrt(var + GROUP_NORM_EPS) * lnx_w + lnx_b
        o_ref[pl.ds(t0, C), lanes] = ((yn + bonus) * gate).astype(o_ref.dtype)

    def chunk(j, carry):
        t0 = pl.multiple_of(j * C, C)
        zl = zl_ref[pl.ds(t0, C), :]
        zw = zl[:, 0:LORA_SLOT].astype(F32)
        za = zl[:, LORA_SLOT:2 * LORA_SLOT]
        zg = zl[:, 2 * LORA_SLOT:].astype(F32)
        u = vec_ref[0:1, :] + _dot(jnp.tanh(zw).astype(BF16), wdec_ref[...])
        a = _sigmoid(vec_ref[1:2, :] + _dot(za, wicl_ref[...]))
        gate = _dot(_sigmoid(zg).astype(BF16), wgate_ref[...])
        groups = []
        for g in range(n_groups):
            lanes = slice(g * G, (g + 1) * G)
            groups.append(group_chunk(g, t0, u[:, lanes], a[:, lanes], gate[:, lanes]))
        for _ in itertools.zip_longest(*groups):
            pass
        return carry

    lax.fori_loop(0, n_chunks, chunk, 0)


def _rwkv(proj, zl, wdec, wicl, wgate, vec, *, batch, seq_len, r_col, k_col, v_col, tb):
    G = LANE_GROUP
    width = wdec.shape[1]
    n_groups = width // G
    n_tb = seq_len // tb
    kern = functools.partial(_rwkv_kernel, n_chunks=tb // CHUNK, n_groups=n_groups)

    def col(base):
        return pl.BlockSpec((tb, width), lambda b, s: (b * n_tb + s, base // width))

    full = lambda a: pl.BlockSpec(a.shape, lambda b, s: (0, 0))
    return pl.pallas_call(
        kern,
        grid=(batch, n_tb),
        in_specs=[
            col(r_col), col(k_col), col(v_col),
            pl.BlockSpec((tb, zl.shape[1]), lambda b, s: (b * n_tb + s, 0)),
            full(wdec), full(wicl), full(wgate), full(vec),
        ],
        out_specs=pl.BlockSpec((tb, width), lambda b, s: (b * n_tb + s, 0)),
        out_shape=jax.ShapeDtypeStruct((batch * seq_len, width), BF16),
        scratch_shapes=[pltpu.VMEM((n_groups, G, G), F32)],
        compiler_params=_params("arbitrary", "arbitrary"),
        name="rwkv_time_mix",
    )(proj, proj, proj, zl, wdec, wicl, wgate, vec)


def _band_attn_kernel(q_ref, k_ref, v_ref, bias_ref, o_ref, kpad, vpad, *, seq_len):
    G, H, QB, KW = LANE_GROUP, HEADS_PER_GROUP, Q_BLOCK, KEY_WINDOW
    pad = KW - QB
    kpad[0:pad, :] = jnp.zeros((pad, G), BF16)
    vpad[0:pad, :] = jnp.zeros((pad, G), BF16)
    kpad[pad:, :] = k_ref[...]
    vpad[pad:, :] = v_ref[...]

    lane_head = lax.broadcasted_iota(jnp.int32, (QB, G), 1) >> 6
    key_idx = lax.broadcasted_iota(jnp.int32, (H * QB, KW), 1)
    scale = HEAD_DIM ** -0.5

    def block(qb, carry):
        q0 = pl.multiple_of(qb * QB, QB)
        qv = q_ref[pl.ds(q0, QB), :].astype(F32)
        kw = kpad[pl.ds(q0, KW), :]
        vw = vpad[pl.ds(q0, KW), :]
        qs = jnp.concatenate(
            [jnp.where(lane_head == h, qv, 0.0) for h in range(H)], axis=0).astype(BF16)
        s = _dot_nt(qs, kw) * scale + bias_ref[...].reshape(H * QB, KW)
        s = jnp.where(key_idx + (q0 - pad) >= 0, s, MASK_VALUE)
        m = jnp.max(s, axis=-1, keepdims=True)
        p = jnp.exp(s - m)
        denom = jnp.sum(p, axis=-1, keepdims=True)
        o = _dot(p.astype(BF16), vw) / denom
        out = jnp.where(lane_head == 0, o[0:QB], 0.0)
        for h in range(1, H):
            out = out + jnp.where(lane_head == h, o[h * QB:(h + 1) * QB], 0.0)
        o_ref[pl.ds(q0, QB), :] = out.astype(o_ref.dtype)
        return carry

    lax.fori_loop(0, seq_len // QB, block, 0)


def _band_attn(proj, bias, *, batch, seq_len, q_col, k_col, v_col, width):
    G, H = LANE_GROUP, HEADS_PER_GROUP
    n_groups = width // G
    kern = functools.partial(_band_attn_kernel, seq_len=seq_len)

    def col(base):
        return pl.BlockSpec((seq_len, G), lambda b, g: (b, base // G + g))

    return pl.pallas_call(
        kern,
        grid=(batch, n_groups),
        in_specs=[
            col(q_col), col(k_col), col(v_col),
            pl.BlockSpec((H, Q_BLOCK, KEY_WINDOW), lambda b, g: (g, 0, 0)),
        ],
        out_specs=pl.BlockSpec((seq_len, G), lambda b, g: (b, g)),
        out_shape=jax.ShapeDtypeStruct((batch * seq_len, width), BF16),
        scratch_shapes=[
            pltpu.VMEM((seq_len + KEY_WINDOW - Q_BLOCK, G), BF16),
            pltpu.VMEM((seq_len + KEY_WINDOW - Q_BLOCK, G), BF16),
        ],
        compiler_params=_params("arbitrary", "arbitrary"),
        name="band_attention",
    )(proj, proj, proj, bias)


def _band_bias(rel_bias):
    heads, table = rel_bias.shape
    n_dist = KEY_WINDOW + Q_BLOCK - 1
    far = n_dist - (CHUNK + table)
    by_dist = jnp.concatenate([
        jnp.broadcast_to(rel_bias[:, :1], (heads, CHUNK)), rel_bias,
        jnp.broadcast_to(rel_bias[:, -1:], (heads, far))], axis=1).astype(F32)
    period = jnp.pad(by_dist[:, ::-1], ((0, 0), (0, 1)))
    skew = jnp.tile(period, (1, Q_BLOCK))[:, :Q_BLOCK * n_dist].reshape(heads, Q_BLOCK, n_dist)
    bias = skew[:, :, Q_BLOCK - 1:]
    qc = jnp.arange(Q_BLOCK)[:, None] // CHUNK
    kc = jnp.arange(KEY_WINDOW)[None, :] // CHUNK
    band = (kc >= qc) & (kc <= qc + LEFT_CHUNKS)
    return jnp.where(band[None], bias, MASK_VALUE)


def _merge_kernel(x_ref, ya_ref, yb_ref, za_ref, zb_ref, wa_ref, wb_ref, wo_ref, g_ref, o_ref):
    mixed = (_sigmoid(za_ref[...].astype(F32)) * _dot(ya_ref[...], wa_ref[...])
             + _sigmoid(zb_ref[...].astype(F32)) * _dot(yb_ref[...], wb_ref[...]))
    y = _dot(mixed.astype(BF16), wo_ref[...])
    o_ref[...] = x_ref[...] + _rms_norm(y, g_ref[...])


def _merge(x2d, ya, yb, proj, wa, wb, wo, gain, *, gate_col, bm):
    t, d = x2d.shape
    row = lambda c: pl.BlockSpec((bm, d), lambda m: (m, c))
    full = lambda a: pl.BlockSpec(a.shape, lambda m: (0, 0))
    return pl.pallas_call(
        _merge_kernel,
        grid=(t // bm,),
        in_specs=[row(0), row(0), row(0), row(gate_col // d), row(gate_col // d + 1),
                  full(wa), full(wb), full(wo), full(gain)],
        out_specs=row(0),
        out_shape=jax.ShapeDtypeStruct((t, d), F32),
        compiler_params=_params("arbitrary"),
        name="gated_merge",
    )(x2d, ya, yb, proj, proj, wa, wb, wo, gain)


def _cross_attn_kernel(x_ref, kv_ref, wq_ref, wo_ref, gpre_ref, gpost_ref, o_ref, *, n_heads):
    x = x_ref[...]
    h = _rms_norm(x, gpre_ref[...]).astype(BF16)
    q = _dot(h, wq_ref[...]).astype(BF16)
    width = q.shape[1]
    hd = width // n_heads
    scale = hd ** -0.5
    outs = []
    for i in range(n_heads):
        qh = q[:, i * hd:(i + 1) * hd]
        kh = kv_ref[:, i * hd:(i + 1) * hd]
        vh = kv_ref[:, width + i * hd:width + (i + 1) * hd]
        s = _dot_nt(qh, kh) * scale
        m = jnp.max(s, axis=-1, keepdims=True)
        p = jnp.exp(s - m)
        denom = jnp.sum(p, axis=-1, keepdims=True)
        outs.append((_dot(p.astype(BF16), vh) / denom).astype(BF16))
    o = jnp.concatenate(outs, axis=-1)
    y = _dot(o, wo_ref[...])
    o_ref[...] = x + _rms_norm(y, gpost_ref[...])


def _cross_attn(x2d, kv, wq, wo, gpre, gpost, *, seq_len, mem_tokens, n_heads, bm):
    t, d = x2d.shape
    tiles_per_seq = seq_len // bm
    full = lambda a: pl.BlockSpec(a.shape, lambda m: (0, 0))
    kern = functools.partial(_cross_attn_kernel, n_heads=n_heads)
    return pl.pallas_call(
        kern,
        grid=(t // bm,),
        in_specs=[
            pl.BlockSpec((bm, d), lambda m: (m, 0)),
            pl.BlockSpec((mem_tokens, kv.shape[1]), lambda m: (m // tiles_per_seq, 0)),
            full(wq), full(wo), full(gpre), full(gpost),
        ],
        out_specs=pl.BlockSpec((bm, d), lambda m: (m, 0)),
        out_shape=jax.ShapeDtypeStruct((t, d), F32),
        compiler_params=_params("arbitrary"),
        name="cross_attention",
    )(x2d, kv, wq, wo, gpre, gpost)


def _ffn_kernel(x_ref, wg_ref, wu_ref, wo_ref, gpre_ref, gpost_ref, o_ref):
    x = x_ref[...]
    h = _rms_norm(x, gpre_ref[...]).astype(BF16)
    acc = jnp.zeros(x.shape, F32)
    for j in range(wg_ref.shape[0]):
        gate = _dot(h, wg_ref[j])
        up = _dot(h, wu_ref[j])
        act = (gate * _sigmoid(gate) * up).astype(BF16)
        acc = acc + _dot(act, wo_ref[j])
    o_ref[...] = x + _rms_norm(acc, gpost_ref[...])


def _ffn(x2d, wg, wu, wo, gpre, gpost, *, bm):
    t, d = x2d.shape
    resident = lambda a: pl.BlockSpec(a.shape, lambda m: (0,) * a.ndim,
                                      pipeline_mode=pl.Buffered(1))
    return pl.pallas_call(
        _ffn_kernel,
        grid=(t // bm,),
        in_specs=[
            pl.BlockSpec((bm, d), lambda m: (m, 0)),
            resident(wg), resident(wu), resident(wo), resident(gpre), resident(gpost),
        ],
        out_specs=pl.BlockSpec((bm, d), lambda m: (m, 0)),
        out_shape=jax.ShapeDtypeStruct((t, d), F32),
        compiler_params=_params("arbitrary"),
        name="swiglu_ffn",
    )(x2d, wg, wu, wo, gpre, gpost)


def _pad_to(a, rows=None, cols=None):
    r = a.shape[0] if rows is None else rows
    c = a.shape[1] if cols is None else cols
    return jnp.pad(a, ((0, r - a.shape[0]), (0, c - a.shape[1])))


def _layer(x2d, mem2d, p, *, batch, seq_len):
    d = x2d.shape[1]
    width = p["decay_up"].shape[1]
    n_dec, n_icl, n_gate = p["decay_up"].shape[0], p["iclr_up"].shape[0], p["gate_up"].shape[0]
    gate_slot = -(-n_gate // LORA_SLOT) * LORA_SLOT
    rwkv_in = 3 * width + n_dec + n_icl + n_gate
    row = lambda v: v.reshape(1, -1).astype(F32)

    w_in, mix = p["w_in"], p["shift_mix"]
    lo = 3 * width
    main_cols = jnp.concatenate([w_in[:, :lo], w_in[:, rwkv_in:]], axis=1)
    main_mix = jnp.concatenate([mix[:lo], jnp.zeros((main_cols.shape[1] - lo,), F32)])
    lora_w = jnp.concatenate([
        _pad_to(w_in[:, lo:lo + n_dec], cols=LORA_SLOT),
        _pad_to(w_in[:, lo + n_dec:lo + n_dec + n_icl], cols=LORA_SLOT),
        _pad_to(w_in[:, lo + n_dec + n_icl:rwkv_in], cols=gate_slot)], axis=1)
    lora_mix = jnp.concatenate([
        jnp.pad(mix[lo:lo + n_dec], (0, LORA_SLOT - n_dec)),
        jnp.pad(mix[lo + n_dec:lo + n_dec + n_icl], (0, LORA_SLOT - n_icl)),
        jnp.pad(mix[lo + n_dec + n_icl:rwkv_in], (0, gate_slot - n_gate))])
    wdec = _pad_to(p["decay_up"], rows=LORA_SLOT).astype(BF16)
    wicl = _pad_to(p["iclr_up"], rows=LORA_SLOT).astype(BF16)
    wgate = _pad_to(p["gate_up"], rows=gate_slot).astype(BF16)
    vec = jnp.stack([p["decay_base"], p["iclr_base"], p["key_norm_scale"], p["key_iclr_scale"],
                     p["bonus_scale"].reshape(-1), p["lnx_w"], p["lnx_b"],
                     jnp.zeros((width,), F32)]).astype(F32)

    bm_proj = min(1024, seq_len)
    proj = _norm_proj(x2d, row(p["g_pre_mix"]), main_cols.astype(BF16), row(main_mix),
                      n_shift=lo // 1024, seq_len=seq_len, bm=bm_proj, bn=1024)
    zl = _norm_proj(x2d, row(p["g_pre_mix"]), lora_w.astype(BF16), row(lora_mix),
                    n_shift=1, seq_len=seq_len, bm=bm_proj, bn=lora_w.shape[1])
    ya = _rwkv(proj, zl, wdec, wicl, wgate, vec, batch=batch, seq_len=seq_len,
               r_col=0, k_col=width, v_col=2 * width, tb=min(1024, seq_len))
    yb = _band_attn(proj, _band_bias(p["rel_bias"]), batch=batch, seq_len=seq_len,
                    q_col=3 * width, k_col=4 * width, v_col=5 * width, width=width)
    x2d = _merge(x2d, ya, yb, proj, p["w_branch_a"].astype(BF16), p["w_branch_b"].astype(BF16),
                 p["w_out"].astype(BF16), row(p["g_post_mix"]), gate_col=6 * width, bm=512)

    mem_tokens = mem2d.shape[0] // batch
    zero_mix = jnp.zeros((1, p["w_kv_mem"].shape[1]), F32)
    kv = _norm_proj(mem2d, row(p["g_mem"]), p["w_kv_mem"].astype(BF16), zero_mix,
                    n_shift=0, seq_len=mem_tokens, bm=min(1024, mem2d.shape[0]),
                    bn=p["w_kv_mem"].shape[1])
    x2d = _cross_attn(x2d, kv, p["w_q_mem"].astype(BF16), p["w_o_mem"].astype(BF16),
                      row(p["g_pre_cross"]), row(p["g_post_cross"]),
                      seq_len=seq_len, mem_tokens=mem_tokens, n_heads=4, bm=512)

    hidden = p["w_ffn_out"].shape[0]
    n_hc = hidden // LANE_GROUP
    w_ffn_in = p["w_ffn_in"].astype(BF16)
    wg = w_ffn_in[:, :hidden].reshape(d, n_hc, LANE_GROUP).transpose(1, 0, 2)
    wu = w_ffn_in[:, hidden:].reshape(d, n_hc, LANE_GROUP).transpose(1, 0, 2)
    wo = p["w_ffn_out"].astype(BF16).reshape(n_hc, LANE_GROUP, d)
    return _ffn(x2d, wg, wu, wo, row(p["g_pre_ffn"]), row(p["g_post_ffn"]), bm=512)


def kernel(x, mem, g_pre_mix, g_post_mix, w_in, shift_mix, decay_base, decay_up, iclr_base, iclr_up, gate_up, key_norm_scale, key_iclr_scale, bonus_scale, lnx_w, lnx_b, rel_bias, w_branch_a, w_branch_b, w_out, g_pre_cross, g_post_cross, g_mem, w_q_mem, w_kv_mem, w_o_mem, g_pre_ffn, g_post_ffn, w_ffn_in, w_ffn_out):
    batch, seq_len, d = x.shape
    stacked = dict(
        g_pre_mix=g_pre_mix, g_post_mix=g_post_mix, w_in=w_in, shift_mix=shift_mix,
        decay_base=decay_base, decay_up=decay_up, iclr_base=iclr_base, iclr_up=iclr_up,
        gate_up=gate_up, key_norm_scale=key_norm_scale, key_iclr_scale=key_iclr_scale,
        bonus_scale=bonus_scale, lnx_w=lnx_w, lnx_b=lnx_b, rel_bias=rel_bias,
        w_branch_a=w_branch_a, w_branch_b=w_branch_b, w_out=w_out, g_pre_cross=g_pre_cross,
        g_post_cross=g_post_cross, g_mem=g_mem, w_q_mem=w_q_mem, w_kv_mem=w_kv_mem,
        w_o_mem=w_o_mem, g_pre_ffn=g_pre_ffn, g_post_ffn=g_post_ffn, w_ffn_in=w_ffn_in,
        w_ffn_out=w_ffn_out)
    x2d = x.reshape(batch * seq_len, d)
    mem2d = mem.reshape(-1, d)
    for layer in range(g_pre_mix.shape[0]):
        x2d = _layer(x2d, mem2d, {k: v[layer] for k, v in stacked.items()},
                     batch=batch, seq_len=seq_len)
    return x2d.reshape(batch, seq_len, d)
```

```python
import functools
import itertools

import jax
import jax.numpy as jnp
from jax import lax
from jax.experimental import pallas as pl
from jax.experimental.pallas import tpu as pltpu

F32 = jnp.float32
BF16 = jnp.bfloat16

NORM_EPS = 1e-6
GROUP_NORM_EPS = 64e-5
MASK_VALUE = -1e30

HEAD_DIM = 64
LANE_GROUP = 256
HEADS_PER_GROUP = LANE_GROUP // HEAD_DIM
CHUNK = 64
LEFT_CHUNKS = 8
REL_CLIP = 128
Q_BLOCK = 2 * CHUNK
KEY_WINDOW = LEFT_CHUNKS * CHUNK + Q_BLOCK
LORA_SLOT = 128
VMEM_LIMIT = 48 * 1024 * 1024


def _dot(a, b):
    return jnp.dot(a, b, preferred_element_type=F32)


def _dot_nt(a, b):
    return lax.dot_general(a, b, (((1,), (1,)), ((), ())), preferred_element_type=F32)


def _sigmoid(u):
    return 1.0 / (1.0 + jnp.exp(-u))


def _rms_norm(xf, gain):
    ms = jnp.mean(xf * xf, axis=-1, keepdims=True)
    return xf * lax.rsqrt(ms + NORM_EPS) * gain


def _params(*sem):
    return pltpu.CompilerParams(dimension_semantics=sem, vmem_limit_bytes=VMEM_LIMIT)


def _norm_proj_kernel(x_ref, g_ref, w_ref, mix_ref, o_ref, h_scr, carry_scr, *, n_shift, seq_tiles):
    m = pl.program_id(0)
    n = pl.program_id(1)

    @pl.when(n == 0)
    def _():
        h_scr[...] = _rms_norm(x_ref[...], g_ref[...]).astype(BF16)

    p = _dot(h_scr[...], w_ref[...])
    bm = p.shape[0]

    def plain():
        o_ref[...] = p.astype(o_ref.dtype)

    def shifted():
        @pl.when(m % seq_tiles == 0)
        def _():
            carry_scr[n] = jnp.zeros(carry_scr.shape[1:], F32)

        row = lax.broadcasted_iota(jnp.int32, p.shape, 0)
        prev = jnp.where(row == 0, carry_scr[n][7:8, :], pltpu.roll(p, 1, 0))
        o_ref[...] = (p + (prev - p) * mix_ref[...]).astype(o_ref.dtype)
        carry_scr[n] = p[bm - 8:, :]

    if n_shift == 0:
        plain()
    else:
        pl.when(n < n_shift)(shifted)
        pl.when(n >= n_shift)(plain)


def _norm_proj(x2d, gain, w, mix, *, n_shift, seq_len, bm, bn):
    t, d = x2d.shape
    n_out = w.shape[1]
    kern = functools.partial(_norm_proj_kernel, n_shift=n_shift, seq_tiles=seq_len // bm)
    return pl.pallas_call(
        kern,
        grid=(t // bm, n_out // bn),
        in_specs=[
            pl.BlockSpec((bm, d), lambda m, n: (m, 0)),
            pl.BlockSpec((1, d), lambda m, n: (0, 0)),
            pl.BlockSpec((d, bn), lambda m, n: (0, n)),
            pl.BlockSpec((1, bn), lambda m, n: (0, n)),
        ],
        out_specs=pl.BlockSpec((bm, bn), lambda m, n: (m, n)),
        out_shape=jax.ShapeDtypeStruct((t, n_out), BF16),
        scratch_shapes=[
            pltpu.VMEM((bm, d), BF16),
            pltpu.VMEM((max(n_shift, 1), 8, bn), F32),
        ],
        compiler_params=_params("arbitrary", "arbitrary"),
        name="norm_proj",
    )(x2d, gain, w, mix)


def _rwkv_kernel(r_ref, k_ref, v_ref, zl_ref, wdec_ref, wicl_ref, wgate_ref, vec_ref,
                 o_ref, st_scr, *, n_chunks, n_groups):
    C, G, H = CHUNK, LANE_GROUP, HEADS_PER_GROUP

    tt = lax.broadcasted_iota(jnp.int32, (C, G), 0)
    ss = lax.broadcasted_iota(jnp.int32, (C, G), 1) & (C - 1)
    m_strict = ss < tt
    m_incl = ss <= tt
    eye_c = jnp.where(ss == tt, 1.0, 0.0).astype(F32)
    r0 = lax.broadcasted_iota(jnp.int32, (G, G), 0)
    c0 = lax.broadcasted_iota(jnp.int32, (G, G), 1)
    blk = (r0 >> 6) == (c0 >> 6)
    ones_bd = jnp.where(blk, 1.0, 0.0).astype(BF16)
    eye_g = jnp.where(r0 == c0, 1.0, 0.0).astype(F32)
    tr = lax.broadcasted_iota(jnp.int32, (C, C), 0)
    tc = lax.broadcasted_iota(jnp.int32, (C, C), 1)
    tri = jnp.where(tc <= tr, 1.0, 0.0).astype(BF16)

    def bd(xc):
        return jnp.where(blk, jnp.concatenate([xc] * H, axis=0), 0.0).astype(BF16)

    def seg_sum(xc):
        return _dot(xc.astype(BF16), ones_bd)

    @pl.when(pl.program_id(1) == 0)
    def _():
        st_scr[...] = jnp.zeros(st_scr.shape, F32)

    def group_chunk(g, t0, u, a, gate):
        lanes = slice(g * G, (g + 1) * G)
        vec = vec_ref[:, lanes]
        kns, kis, bonus_scale = vec[2:3], vec[3:4], vec[4:5]
        lnx_w, lnx_b = vec[5:6], vec[6:7]
        rc = r_ref[pl.ds(t0, C), lanes].astype(F32)
        kc = k_ref[pl.ds(t0, C), lanes].astype(F32)
        vc = v_ref[pl.ds(t0, C), lanes].astype(F32)

        softplus_neg_u = jnp.maximum(-u, 0.0) + jnp.log(1.0 + jnp.exp(-jnp.abs(u)))
        ld = -jnp.exp(-softplus_neg_u - 0.5)
        ld_hi = ld.astype(BF16)
        ld_lo = (ld - ld_hi.astype(F32)).astype(BF16)
        cum = _dot(tri, ld_hi) + _dot(tri, ld_lo)
        kk = kc * kns
        kk_ss = seg_sum(kk * kk)
        k2 = kc * (1.0 + (a - 1.0) * kis)
        bonus_s = seg_sum(rc * k2 * bonus_scale)
        yield

        kk = kk * lax.rsqrt(jnp.maximum(kk_ss, 1e-24))
        beta = kk * a
        bonus = bonus_s * vc
        cum_end = cum[C - 1:C, :]
        p_t = jnp.exp(cum)
        p_inv = jnp.exp(-cum)
        p_prev = jnp.exp(cum - ld)
        p_end = jnp.exp(cum_end - cum)
        rt, kt, bt, kkt = rc * p_t, k2 * p_inv, beta * p_inv, kk * p_prev
        kh, bh = k2 * p_end, beta * p_end

        lhs = jnp.concatenate([kkt, rt], axis=0).astype(BF16)
        ab = _dot_nt(lhs, bd(bt))
        ak = _dot_nt(lhs, bd(kt))
        st16 = st_scr[g].astype(BF16)
        ws = _dot(lhs, st16)
        yield
        a1 = jnp.where(m_strict, ab[:C], 0.0)
        a4 = jnp.where(m_incl, ab[C:], 0.0)
        a2 = jnp.where(m_strict, ak[:C], 0.0)
        a3 = jnp.where(m_incl, ak[C:], 0.0)

        x = -a1
        t_inv = eye_c + x
        xn = _dot(x.astype(BF16), bd(x))
        av = _dot(jnp.concatenate([a2, a3], axis=0).astype(BF16), bd(vc))
        yield
        rhs_u = ws[:C] + av[:C]
        y_part = ws[C:] + av[C:]
        for level in range(1, 6):
            xbd = bd(xn)
            if level < 5:
                both = _dot(jnp.concatenate([t_inv, xn], axis=0).astype(BF16), xbd)
                yield
                t_inv = t_inv + both[:C]
                xn = both[C:]
            else:
                t_inv = t_inv + _dot(t_inv.astype(BF16), xbd)
                yield

        u_mat = _dot(t_inv.astype(BF16), bd(rhs_u))
        yield
        y = y_part - _dot(a4.astype(BF16), bd(u_mat))
        kb_t = jnp.concatenate([kh, -bh], axis=0).T.astype(BF16)
        lhs_s = jnp.concatenate([(eye_g * p_t[C - 1:C, :]).astype(BF16), kb_t], axis=1)
        rhs_s = jnp.concatenate([st16, vc.astype(BF16), u_mat.astype(BF16)], axis=0)
        st_scr[g] = jnp.where(blk, _dot(lhs_s, rhs_s), 0.0)
        yield

        mu = seg_sum(y) * (1.0 / HEAD_DIM)
        yield
        d = y - mu
        var = seg_sum(d * d) * (1.0 / HEAD_DIM)
        yield
        yn = d * lax.rsqrt(var + GROUP_NORM_EPS) * lnx_w + lnx_b
        o_ref[pl.ds(t0, C), lanes] = ((yn + bonus) * gate).astype(o_ref.dtype)

    def chunk(j, carry):
        t0 = pl.multiple_of(j * C, C)
        zl = zl_ref[pl.ds(t0, C), :]
        zw = zl[:, 0:LORA_SLOT].astype(F32)
        za = zl[:, LORA_SLOT:2 * LORA_SLOT]
        zg = zl[:, 2 * LORA_SLOT:].astype(F32)
        u = vec_ref[0:1, :] + _dot(jnp.tanh(zw).astype(BF16), wdec_ref[...])
        a = _sigmoid(vec_ref[1:2, :] + _dot(za, wicl_ref[...]))
        gate = _dot(_sigmoid(zg).astype(BF16), wgate_ref[...])
        groups = []
        for g in range(n_groups):
            lanes = slice(g * G, (g + 1) * G)
            groups.append(group_chunk(g, t0, u[:, lanes], a[:, lanes], gate[:, lanes]))
        for _ in itertools.zip_longest(*groups):
            pass
        return carry

    lax.fori_loop(0, n_chunks, chunk, 0)


def _rwkv(proj, zl, wdec, wicl, wgate, vec, *, batch, seq_len, r_col, k_col, v_col, tb):
    G = LANE_GROUP
    width = wdec.shape[1]
    n_groups = width // G
    n_tb = seq_len // tb
    kern = functools.partial(_rwkv_kernel, n_chunks=tb // CHUNK, n_groups=n_groups)

    def col(base):
        return pl.BlockSpec((tb, width), lambda b, s: (b * n_tb + s, base // width))

    full = lambda a: pl.BlockSpec(a.shape, lambda b, s: (0, 0))
    return pl.pallas_call(
        kern,
        grid=(batch, n_tb),
        in_specs=[
            col(r_col), col(k_col), col(v_col),
            pl.BlockSpec((tb, zl.shape[1]), lambda b, s: (b * n_tb + s, 0)),
            full(wdec), full(wicl), full(wgate), full(vec),
        ],
        out_specs=pl.BlockSpec((tb, width), lambda b, s: (b * n_tb + s, 0)),
        out_shape=jax.ShapeDtypeStruct((batch * seq_len, width), BF16),
        scratch_shapes=[pltpu.VMEM((n_groups, G, G), F32)],
        compiler_params=_params("arbitrary", "arbitrary"),
        name="rwkv_time_mix",
    )(proj, proj, proj, zl, wdec, wicl, wgate, vec)


def _band_attn_kernel(q_ref, k_ref, v_ref, bias_ref, o_ref, kpad, vpad, *, seq_len):
    G, H, QB, KW = LANE_GROUP, HEADS_PER_GROUP, Q_BLOCK, KEY_WINDOW
    pad = KW - QB
    kpad[0:pad, :] = jnp.zeros((pad, G), BF16)
    vpad[0:pad, :] = jnp.zeros((pad, G), BF16)
    kpad[pad:, :] = k_ref[...]
    vpad[pad:, :] = v_ref[...]

    lane_head = lax.broadcasted_iota(jnp.int32, (QB, G), 1) >> 6
    key_idx = lax.broadcasted_iota(jnp.int32, (H * QB, KW), 1)
    scale = HEAD_DIM ** -0.5
    per_iter = pad // QB

    def block(q0, first_keys):
        qv = q_ref[pl.ds(q0, QB), :].astype(F32) * scale
        kw = kpad[pl.ds(q0, KW), :]
        vw = vpad[pl.ds(q0, KW), :]
        qs = jnp.concatenate(
            [jnp.where(lane_head == h, qv, 0.0) for h in range(H)], axis=0).astype(BF16)
        s = _dot_nt(qs, kw)
        yield
        s = s + bias_ref[...].reshape(H * QB, KW)
        if first_keys:
            s = jnp.where(key_idx >= pad - q0, s, MASK_VALUE)
        m = jnp.max(s, axis=-1, keepdims=True)
        p = jnp.exp(s - m)
        denom = jnp.sum(p, axis=-1, keepdims=True)
        o = _dot(p.astype(BF16), vw)
        yield
        o = o / denom
        out = jnp.where(lane_head == 0, o[0:QB], 0.0)
        for h in range(1, H):
            out = out + jnp.where(lane_head == h, o[h * QB:(h + 1) * QB], 0.0)
        o_ref[pl.ds(q0, QB), :] = out.astype(o_ref.dtype)

    def run(blocks):
        for _ in itertools.zip_longest(*blocks):
            pass

    run([block(i * QB, True) for i in range(per_iter)])

    def later(it, carry):
        base = it * (per_iter * QB)
        run([block(pl.multiple_of(base + i * QB, QB), False) for i in range(per_iter)])
        return carry

    lax.fori_loop(1, seq_len // (per_iter * QB), later, 0)


def _band_attn(proj, bias, *, batch, seq_len, q_col, k_col, v_col, width):
    G, H = LANE_GROUP, HEADS_PER_GROUP
    n_groups = width // G
    kern = functools.partial(_band_attn_kernel, seq_len=seq_len)

    def col(base):
        return pl.BlockSpec((seq_len, G), lambda b, g: (b, base // G + g))

    return pl.pallas_call(
        kern,
        grid=(batch, n_groups),
        in_specs=[
            col(q_col), col(k_col), col(v_col),
            pl.BlockSpec((H, Q_BLOCK, KEY_WINDOW), lambda b, g: (g, 0, 0)),
        ],
        out_specs=pl.BlockSpec((seq_len, G), lambda b, g: (b, g)),
        out_shape=jax.ShapeDtypeStruct((batch * seq_len, width), BF16),
        scratch_shapes=[
            pltpu.VMEM((seq_len + KEY_WINDOW - Q_BLOCK, G), BF16),
            pltpu.VMEM((seq_len + KEY_WINDOW - Q_BLOCK, G), BF16),
        ],
        compiler_params=_params("arbitrary", "arbitrary"),
        name="band_attention",
    )(proj, proj, proj, bias)


def _band_bias(rel_bias):
    heads, table = rel_bias.shape
    n_dist = KEY_WINDOW + Q_BLOCK - 1
    far = n_dist - (CHUNK + table)
    by_dist = jnp.concatenate([
        jnp.broadcast_to(rel_bias[:, :1], (heads, CHUNK)), rel_bias,
        jnp.broadcast_to(rel_bias[:, -1:], (heads, far))], axis=1).astype(F32)
    period = jnp.pad(by_dist[:, ::-1], ((0, 0), (0, 1)))
    skew = jnp.tile(period, (1, Q_BLOCK))[:, :Q_BLOCK * n_dist].reshape(heads, Q_BLOCK, n_dist)
    bias = skew[:, :, Q_BLOCK - 1:]
    qc = jnp.arange(Q_BLOCK)[:, None] // CHUNK
    kc = jnp.arange(KEY_WINDOW)[None, :] // CHUNK
    band = (kc >= qc) & (kc <= qc + LEFT_CHUNKS)
    return jnp.where(band[None], bias, MASK_VALUE)


def _merge_kernel(x_ref, ya_ref, yb_ref, za_ref, zb_ref, wa_ref, wb_ref, wo_ref, g_ref, o_ref):
    mixed = (_sigmoid(za_ref[...].astype(F32)) * _dot(ya_ref[...], wa_ref[...])
             + _sigmoid(zb_ref[...].astype(F32)) * _dot(yb_ref[...], wb_ref[...]))
    y = _dot(mixed.astype(BF16), wo_ref[...])
    o_ref[...] = x_ref[...] + _rms_norm(y, g_ref[...])


def _merge(x2d, ya, yb, proj, wa, wb, wo, gain, *, gate_col, bm):
    t, d = x2d.shape
    row = lambda c: pl.BlockSpec((bm, d), lambda m: (m, c))
    full = lambda a: pl.BlockSpec(a.shape, lambda m: (0, 0))
    return pl.pallas_call(
        _merge_kernel,
        grid=(t // bm,),
        in_specs=[row(0), row(0), row(0), row(gate_col // d), row(gate_col // d + 1),
                  full(wa), full(wb), full(wo), full(gain)],
        out_specs=row(0),
        out_shape=jax.ShapeDtypeStruct((t, d), F32),
        compiler_params=_params("arbitrary"),
        name="gated_merge",
    )(x2d, ya, yb, proj, proj, wa, wb, wo, gain)


def _cross_attn_kernel(x_ref, kv_ref, wq_ref, wo_ref, gpre_ref, gpost_ref, o_ref, *, n_heads):
    x = x_ref[...]
    h = _rms_norm(x, gpre_ref[...]).astype(BF16)
    q = _dot(h, wq_ref[...]).astype(BF16)
    width = q.shape[1]
    hd = width // n_heads
    scale = hd ** -0.5
    outs = []
    for i in range(n_heads):
        qh = q[:, i * hd:(i + 1) * hd]
        kh = kv_ref[:, i * hd:(i + 1) * hd]
        vh = kv_ref[:, width + i * hd:width + (i + 1) * hd]
        s = _dot_nt(qh, kh) * scale
        m = jnp.max(s, axis=-1, keepdims=True)
        p = jnp.exp(s - m)
        denom = jnp.sum(p, axis=-1, keepdims=True)
        outs.append((_dot(p.astype(BF16), vh) / denom).astype(BF16))
    o = jnp.concatenate(outs, axis=-1)
    y = _dot(o, wo_ref[...])
    o_ref[...] = x + _rms_norm(y, gpost_ref[...])


def _cross_attn(x2d, kv, wq, wo, gpre, gpost, *, seq_len, mem_tokens, n_heads, bm):
    t, d = x2d.shape
    tiles_per_seq = seq_len // bm
    full = lambda a: pl.BlockSpec(a.shape, lambda m: (0, 0))
    kern = functools.partial(_cross_attn_kernel, n_heads=n_heads)
    return pl.pallas_call(
        kern,
        grid=(t // bm,),
        in_specs=[
            pl.BlockSpec((bm, d), lambda m: (m, 0)),
            pl.BlockSpec((mem_tokens, kv.shape[1]), lambda m: (m // tiles_per_seq, 0)),
            full(wq), full(wo), full(gpre), full(gpost),
        ],
        out_specs=pl.BlockSpec((bm, d), lambda m: (m, 0)),
        out_shape=jax.ShapeDtypeStruct((t, d), F32),
        compiler_params=_params("arbitrary"),
        name="cross_attention",
    )(x2d, kv, wq, wo, gpre, gpost)


def _ffn_kernel(x_ref, wg_ref, wu_ref, wo_ref, gpre_ref, gpost_ref, o_ref):
    x = x_ref[...]
    h = _rms_norm(x, gpre_ref[...]).astype(BF16)
    acc = jnp.zeros(x.shape, F32)
    for j in range(wg_ref.shape[0]):
        gate = _dot(h, wg_ref[j])
        up = _dot(h, wu_ref[j])
        act = (gate * _sigmoid(gate) * up).astype(BF16)
        acc = acc + _dot(act, wo_ref[j])
    o_ref[...] = x + _rms_norm(acc, gpost_ref[...])


def _ffn(x2d, wg, wu, wo, gpre, gpost, *, bm):
    t, d = x2d.shape
    resident = lambda a: pl.BlockSpec(a.shape, lambda m: (0,) * a.ndim,
                                      pipeline_mode=pl.Buffered(1))
    return pl.pallas_call(
        _ffn_kernel,
        grid=(t // bm,),
        in_specs=[
            pl.BlockSpec((bm, d), lambda m: (m, 0)),
            resident(wg), resident(wu), resident(wo), resident(gpre), resident(gpost),
        ],
        out_specs=pl.BlockSpec((bm, d), lambda m: (m, 0)),
        out_shape=jax.ShapeDtypeStruct((t, d), F32),
        compiler_params=_params("arbitrary"),
        name="swiglu_ffn",
    )(x2d, wg, wu, wo, gpre, gpost)


def _pad_to(a, rows=None, cols=None):
    r = a.shape[0] if rows is None else rows
    c = a.shape[1] if cols is None else cols
    return jnp.pad(a, ((0, r - a.shape[0]), (0, c - a.shape[1])))


def _layer(x2d, mem2d, p, *, batch, seq_len):
    d = x2d.shape[1]
    width = p["decay_up"].shape[1]
    n_dec, n_icl, n_gate = p["decay_up"].shape[0], p["iclr_up"].shape[0], p["gate_up"].shape[0]
    gate_slot = -(-n_gate // LORA_SLOT) * LORA_SLOT
    rwkv_in = 3 * width + n_dec + n_icl + n_gate
    row = lambda v: v.reshape(1, -1).astype(F32)

    w_in, mix = p["w_in"], p["shift_mix"]
    lo = 3 * width
    main_cols = jnp.concatenate([w_in[:, :lo], w_in[:, rwkv_in:]], axis=1)
    main_mix = jnp.concatenate([mix[:lo], jnp.zeros((main_cols.shape[1] - lo,), F32)])
    lora_w = jnp.concatenate([
        _pad_to(w_in[:, lo:lo + n_dec], cols=LORA_SLOT),
        _pad_to(w_in[:, lo + n_dec:lo + n_dec + n_icl], cols=LORA_SLOT),
        _pad_to(w_in[:, lo + n_dec + n_icl:rwkv_in], cols=gate_slot)], axis=1)
    lora_mix = jnp.concatenate([
        jnp.pad(mix[lo:lo + n_dec], (0, LORA_SLOT - n_dec)),
        jnp.pad(mix[lo + n_dec:lo + n_dec + n_icl], (0, LORA_SLOT - n_icl)),
        jnp.pad(mix[lo + n_dec + n_icl:rwkv_in], (0, gate_slot - n_gate))])
    wdec = _pad_to(p["decay_up"], rows=LORA_SLOT).astype(BF16)
    wicl = _pad_to(p["iclr_up"], rows=LORA_SLOT).astype(BF16)
    wgate = _pad_to(p["gate_up"], rows=gate_slot).astype(BF16)
    vec = jnp.stack([p["decay_base"], p["iclr_base"], p["key_norm_scale"], p["key_iclr_scale"],
                     p["bonus_scale"].reshape(-1), p["lnx_w"], p["lnx_b"],
                     jnp.zeros((width,), F32)]).astype(F32)

    bm_proj = min(1024, seq_len)
    proj = _norm_proj(x2d, row(p["g_pre_mix"]), main_cols.astype(BF16), row(main_mix),
                      n_shift=lo // 1024, seq_len=seq_len, bm=bm_proj, bn=1024)
    zl = _norm_proj(x2d, row(p["g_pre_mix"]), lora_w.astype(BF16), row(lora_mix),
                    n_shift=1, seq_len=seq_len, bm=bm_proj, bn=lora_w.shape[1])
    ya = _rwkv(proj, zl, wdec, wicl, wgate, vec, batch=batch, seq_len=seq_len,
               r_col=0, k_col=width, v_col=2 * width, tb=min(1024, seq_len))
    yb = _band_attn(proj, _band_bias(p["rel_bias"]), batch=batch, seq_len=seq_len,
                    q_col=3 * width, k_col=4 * width, v_col=5 * width, width=width)
    x2d = _merge(x2d, ya, yb, proj, p["w_branch_a"].astype(BF16), p["w_branch_b"].astype(BF16),
                 p["w_out"].astype(BF16), row(p["g_post_mix"]), gate_col=6 * width, bm=512)

    mem_tokens = mem2d.shape[0] // batch
    zero_mix = jnp.zeros((1, p["w_kv_mem"].shape[1]), F32)
    kv = _norm_proj(mem2d, row(p["g_mem"]), p["w_kv_mem"].astype(BF16), zero_mix,
                    n_shift=0, seq_len=mem_tokens, bm=min(1024, mem2d.shape[0]),
                    bn=p["w_kv_mem"].shape[1])
    x2d = _cross_attn(x2d, kv, p["w_q_mem"].astype(BF16), p["w_o_mem"].astype(BF16),
                      row(p["g_pre_cross"]), row(p["g_post_cross"]),
                      seq_len=seq_len, mem_tokens=mem_tokens, n_heads=4, bm=512)

    hidden = p["w_ffn_out"].shape[0]
    n_hc = hidden // LANE_GROUP
    w_ffn_in = p["w_ffn_in"].astype(BF16)
    wg = w_ffn_in[:, :hidden].reshape(d, n_hc, LANE_GROUP).transpose(1, 0, 2)
    wu = w_ffn_in[:, hidden:].reshape(d, n_hc, LANE_GROUP).transpose(1, 0, 2)
    wo = p["w_ffn_out"].astype(BF16).reshape(n_hc, LANE_GROUP, d)
    return _ffn(x2d, wg, wu, wo, row(p["g_pre_ffn"]), row(p["g_post_ffn"]), bm=512)


def kernel(x, mem, g_pre_mix, g_post_mix, w_in, shift_mix, decay_base, decay_up, iclr_base, iclr_up, gate_up, key_norm_scale, key_iclr_scale, bonus_scale, lnx_w, lnx_b, rel_bias, w_branch_a, w_branch_b, w_out, g_pre_cross, g_post_cross, g_mem, w_q_mem, w_kv_mem, w_o_mem, g_pre_ffn, g_post_ffn, w_ffn_in, w_ffn_out):
    batch, seq_len, d = x.shape
    stacked = dict(
        g_pre_mix=g_pre_mix, g_post_mix=g_post_mix, w_in=w_in, shift_mix=shift_mix,
        decay_base=decay_base, decay_up=decay_up, iclr_base=iclr_base, iclr_up=iclr_up,
        gate_up=gate_up, key_norm_scale=key_norm_scale, key_iclr_scale=key_iclr_scale,
        bonus_scale=bonus_scale, lnx_w=lnx_w, lnx_b=lnx_b, rel_bias=rel_bias,
        w_branch_a=w_branch_a, w_branch_b=w_branch_b, w_out=w_out, g_pre_cross=g_pre_cross,
        g_post_cross=g_post_cross, g_mem=g_mem, w_q_mem=w_q_mem, w_kv_mem=w_kv_mem,
        w_o_mem=w_o_mem, g_pre_ffn=g_pre_ffn, g_post_ffn=g_post_ffn, w_ffn_in=w_ffn_in,
        w_ffn_out=w_ffn_out)
    x2d = x.reshape(batch * seq_len, d)
    mem2d = mem.reshape(-1, d)
    for layer in range(g_pre_mix.shape[0]):
        x2d = _layer(x2d, mem2d, {k: v[layer] for k, v in stacked.items()},
                     batch=batch, seq_len=seq_len)
    return x2d.reshape(batch, seq_len, d)
```

```python
import functools
import itertools

import jax
import jax.numpy as jnp
from jax import lax
from jax.experimental import pallas as pl
from jax.experimental.pallas import tpu as pltpu

F32 = jnp.float32
BF16 = jnp.bfloat16

NORM_EPS = 1e-6
GROUP_NORM_EPS = 64e-5
MASK_VALUE = -1e30

HEAD_DIM = 64
LANE_GROUP = 256
HEADS_PER_GROUP = LANE_GROUP // HEAD_DIM
CHUNK = 64
LEFT_CHUNKS = 8
REL_CLIP = 128
Q_BLOCK = 2 * CHUNK
KEY_WINDOW = LEFT_CHUNKS * CHUNK + Q_BLOCK
LORA_SLOT = 128
VMEM_LIMIT = 48 * 1024 * 1024


def _dot(a, b):
    return jnp.dot(a, b, preferred_element_type=F32)


def _dot_nt(a, b):
    return lax.dot_general(a, b, (((1,), (1,)), ((), ())), preferred_element_type=F32)


def _sigmoid(u):
    return 1.0 / (1.0 + jnp.exp(-u))


def _rms_norm(xf, gain):
    ms = jnp.mean(xf * xf, axis=-1, keepdims=True)
    return xf * lax.rsqrt(ms + NORM_EPS) * gain


def _params(*sem):
    return pltpu.CompilerParams(dimension_semantics=sem, vmem_limit_bytes=VMEM_LIMIT)


def _norm_proj_kernel(x_ref, g_ref, w_ref, mix_ref, o_ref, h_scr, carry_scr, *, shift, seq_tiles):
    m = pl.program_id(0)
    n = pl.program_id(1)

    @pl.when(n == 0)
    def _():
        h_scr[...] = _rms_norm(x_ref[...], g_ref[...]).astype(BF16)

    if shift:
        @pl.when(m % seq_tiles == 0)
        def _():
            carry_scr[n] = jnp.zeros(carry_scr.shape[1:], F32)

    p = _dot(h_scr[...], w_ref[...])
    if shift:
        bm = p.shape[0]
        row = lax.broadcasted_iota(jnp.int32, p.shape, 0)
        prev = jnp.where(row == 0, carry_scr[n][7:8, :], pltpu.roll(p, 1, 0))
        o_ref[...] = (p + (prev - p) * mix_ref[...]).astype(o_ref.dtype)
        carry_scr[n] = p[bm - 8:, :]
    else:
        o_ref[...] = p.astype(o_ref.dtype)


def _norm_proj(x2d, gain, w, mix, *, seq_len, bm, bn):
    t, d = x2d.shape
    n_out = w.shape[1]
    shift = mix is not None
    if not shift:
        mix = jnp.zeros((1, n_out), F32)
    n_carry = n_out // bn if shift else 1
    kern = functools.partial(_norm_proj_kernel, shift=shift, seq_tiles=max(seq_len // bm, 1))
    return pl.pallas_call(
        kern,
        grid=(t // bm, n_out // bn),
        in_specs=[
            pl.BlockSpec((bm, d), lambda m, n: (m, 0)),
            pl.BlockSpec((1, d), lambda m, n: (0, 0)),
            pl.BlockSpec((d, bn), lambda m, n: (0, n)),
            pl.BlockSpec((1, bn), lambda m, n: (0, n)),
        ],
        out_specs=pl.BlockSpec((bm, bn), lambda m, n: (m, n)),
        out_shape=jax.ShapeDtypeStruct((t, n_out), BF16),
        scratch_shapes=[
            pltpu.VMEM((bm, d), BF16),
            pltpu.VMEM((n_carry, 8, bn), F32),
        ],
        compiler_params=_params("arbitrary", "arbitrary"),
        name="norm_proj",
    )(x2d, gain, w, mix)


def _rwkv_kernel(r_ref, k_ref, v_ref, zl_ref, wdec_ref, wicl_ref, wgate_ref, vec_ref,
                 o_ref, st_scr, *, n_chunks, n_groups, n_seqs):
    C, G, H = CHUNK, LANE_GROUP, HEADS_PER_GROUP

    tt = lax.broadcasted_iota(jnp.int32, (C, G), 0)
    ss = lax.broadcasted_iota(jnp.int32, (C, G), 1) & (C - 1)
    m_strict = ss < tt
    m_incl = ss <= tt
    eye_c = jnp.where(ss == tt, 1.0, 0.0).astype(F32)
    r0 = lax.broadcasted_iota(jnp.int32, (G, G), 0)
    c0 = lax.broadcasted_iota(jnp.int32, (G, G), 1)
    blk = (r0 >> 6) == (c0 >> 6)
    ones_bd = jnp.where(blk, 1.0, 0.0).astype(BF16)
    tr = lax.broadcasted_iota(jnp.int32, (C, 2 * C), 0)
    tc = lax.broadcasted_iota(jnp.int32, (C, 2 * C), 1) & (C - 1)
    tri2 = jnp.where(tc <= tr, 1.0, 0.0).astype(BF16)

    def bd(xc):
        return jnp.where(blk, jnp.concatenate([xc] * H, axis=0), 0.0).astype(BF16)

    def seg_sum(xc):
        return _dot(xc.astype(BF16), ones_bd)

    @pl.when(pl.program_id(1) == 0)
    def _():
        st_scr[...] = jnp.zeros(st_scr.shape, F32)

    def group_chunk(b, g, t0, u, a, gate):
        lanes = slice(g * G, (g + 1) * G)
        vec = vec_ref[:, lanes]
        kns, kis, bonus_scale = vec[2:3], vec[3:4], vec[4:5]
        lnx_w, lnx_b = vec[5:6], vec[6:7]
        rc = r_ref[b, pl.ds(t0, C), lanes].astype(F32)
        kc = k_ref[b, pl.ds(t0, C), lanes].astype(F32)
        vc = v_ref[b, pl.ds(t0, C), lanes].astype(F32)

        softplus_neg_u = jnp.maximum(-u, 0.0) + jnp.log(1.0 + jnp.exp(-jnp.abs(u)))
        ld = -jnp.exp(-softplus_neg_u - 0.5)
        ld_hi = ld.astype(BF16)
        ld_lo = (ld - ld_hi.astype(F32)).astype(BF16)
        cum = _dot(tri2, jnp.concatenate([ld_hi, ld_lo], axis=0))
        kk = kc * kns
        kk_ss = seg_sum(kk * kk)
        k2 = kc * (1.0 + (a - 1.0) * kis)
        bonus_s = seg_sum(rc * k2 * bonus_scale)
        yield

        kk = kk * lax.rsqrt(jnp.maximum(kk_ss, 1e-24))
        beta = kk * a
        bonus = bonus_s * vc
        p_t = jnp.exp(cum)
        p_inv = jnp.exp(-cum)
        p_prev = jnp.exp(cum - ld)
        p_last = p_t[C - 1:C, :]
        rt, kt, bt, kkt = rc * p_t, k2 * p_inv, beta * p_inv, kk * p_prev
        kh, bh = kt * p_last, bt * p_last

        lhs = jnp.concatenate([kkt, rt], axis=0).astype(BF16)
        ab = _dot_nt(lhs, bd(bt))
        ak = _dot_nt(lhs, bd(kt))
        st = st_scr[b, g]
        ws = _dot_nt(lhs, st.astype(BF16))
        yield
        a1 = jnp.where(m_strict, ab[:C], 0.0)
        a4 = jnp.where(m_incl, ab[C:], 0.0)
        a2 = jnp.where(m_strict, ak[:C], 0.0)
        a3 = jnp.where(m_incl, ak[C:], 0.0)

        x = -a1
        t_inv = eye_c + x
        xn = _dot(x.astype(BF16), bd(x))
        av = _dot(jnp.concatenate([a2, a3], axis=0).astype(BF16), bd(vc))
        yield
        rhs_u = ws[:C] + av[:C]
        y_part = ws[C:] + av[C:]
        for level in range(1, 6):
            xbd = bd(xn)
            if level < 5:
                both = _dot(jnp.concatenate([t_inv, xn], axis=0).astype(BF16), xbd)
                yield
                t_inv = t_inv + both[:C]
                xn = both[C:]
            else:
                t_inv = t_inv + _dot(t_inv.astype(BF16), xbd)
                yield

        u_mat = _dot(t_inv.astype(BF16), bd(rhs_u))
        yield
        y = y_part - _dot(a4.astype(BF16), bd(u_mat))
        vu_t = jnp.concatenate([vc, -u_mat], axis=0).T.astype(BF16)
        kb = jnp.concatenate([kh, bh], axis=0).astype(BF16)
        st_scr[b, g] = st * p_last + jnp.where(blk, _dot(vu_t, kb), 0.0)
        yield

        mu = seg_sum(y) * (1.0 / HEAD_DIM)
        yield
        d = y - mu
        var = seg_sum(d * d) * (1.0 / HEAD_DIM)
        yield
        yn = d * lax.rsqrt(var + GROUP_NORM_EPS) * lnx_w + lnx_b
        o_ref[b, pl.ds(t0, C), lanes] = ((yn + bonus) * gate(b, g)).astype(o_ref.dtype)

    def chunk(j, carry):
        t0 = pl.multiple_of(j * C, C)

        def lora_in(lo, hi):
            return jnp.concatenate(
                [zl_ref[b, pl.ds(t0, C), lo:hi] for b in range(n_seqs)], axis=0)

        zw = lora_in(0, LORA_SLOT).astype(F32)
        u = vec_ref[0:1, :] + _dot(jnp.tanh(zw).astype(BF16), wdec_ref[...])
        a = _sigmoid(vec_ref[1:2, :] + _dot(lora_in(LORA_SLOT, 2 * LORA_SLOT), wicl_ref[...]))

        gate_all = []

        def gate(b, g):
            if not gate_all:
                zg = lora_in(2 * LORA_SLOT, zl_ref.shape[2]).astype(F32)
                gate_all.append(_dot(_sigmoid(zg).astype(BF16), wgate_ref[...]))
            return gate_all[0][b * C:(b + 1) * C, g * G:(g + 1) * G]

        chains = []
        for b in range(n_seqs):
            rows = slice(b * C, (b + 1) * C)
            for g in range(n_groups):
                lanes = slice(g * G, (g + 1) * G)
                chains.append(group_chunk(b, g, t0, u[rows, lanes], a[rows, lanes], gate))
        for _ in itertools.zip_longest(*chains):
            pass
        return carry

    lax.fori_loop(0, n_chunks, chunk, 0)


def _rwkv(proj, zl, wdec, wicl, wgate, vec, *, batch, seq_len, r_col, k_col, v_col, tb, n_seqs):
    G = LANE_GROUP
    width = wdec.shape[1]
    n_groups = width // G
    kern = functools.partial(_rwkv_kernel, n_chunks=tb // CHUNK, n_groups=n_groups,
                             n_seqs=n_seqs)

    def col(base):
        return pl.BlockSpec((n_seqs, tb, width), lambda b, s: (b, s, base // width))

    full = lambda a: pl.BlockSpec(a.shape, lambda b, s: (0, 0))
    return pl.pallas_call(
        kern,
        grid=(batch // n_seqs, seq_len // tb),
        in_specs=[
            col(r_col), col(k_col), col(v_col),
            pl.BlockSpec((n_seqs, tb, zl.shape[2]), lambda b, s: (b, s, 0)),
            full(wdec), full(wicl), full(wgate), full(vec),
        ],
        out_specs=pl.BlockSpec((n_seqs, tb, width), lambda b, s: (b, s, 0)),
        out_shape=jax.ShapeDtypeStruct((batch, seq_len, width), BF16),
        scratch_shapes=[pltpu.VMEM((n_seqs, n_groups, G, G), F32)],
        compiler_params=_params("arbitrary", "arbitrary"),
        name="rwkv_time_mix",
    )(proj, proj, proj, zl, wdec, wicl, wgate, vec)


def _band_attn_kernel(q_ref, k_ref, v_ref, bias_ref, o_ref, kpad, vpad, *, seq_len):
    G, H, QB, KW = LANE_GROUP, HEADS_PER_GROUP, Q_BLOCK, KEY_WINDOW
    pad = KW - QB
    kpad[0:pad, :] = jnp.zeros((pad, G), BF16)
    vpad[0:pad, :] = jnp.zeros((pad, G), BF16)
    kpad[pad:, :] = k_ref[...]
    vpad[pad:, :] = v_ref[...]

    lane_head = lax.broadcasted_iota(jnp.int32, (QB, G), 1) >> 6
    key_idx = lax.broadcasted_iota(jnp.int32, (H * QB, KW), 1)
    scale = HEAD_DIM ** -0.5
    per_iter = pad // QB

    def block(q0, first_keys):
        qv = q_ref[pl.ds(q0, QB), :].astype(F32) * scale
        kw = kpad[pl.ds(q0, KW), :]
        vw = vpad[pl.ds(q0, KW), :]
        qs = jnp.concatenate(
            [jnp.where(lane_head == h, qv, 0.0) for h in range(H)], axis=0).astype(BF16)
        s = _dot_nt(qs, kw)
        yield
        s = s + bias_ref[...].reshape(H * QB, KW)
        if first_keys:
            s = jnp.where(key_idx >= pad - q0, s, MASK_VALUE)
        m = jnp.max(s, axis=-1, keepdims=True)
        p = jnp.exp(s - m)
        denom = jnp.sum(p, axis=-1, keepdims=True)
        o = _dot(p.astype(BF16), vw)
        yield
        o = o / denom
        out = jnp.where(lane_head == 0, o[0:QB], 0.0)
        for h in range(1, H):
            out = out + jnp.where(lane_head == h, o[h * QB:(h + 1) * QB], 0.0)
        o_ref[pl.ds(q0, QB), :] = out.astype(o_ref.dtype)

    def run(blocks):
        for _ in itertools.zip_longest(*blocks):
            pass

    run([block(i * QB, True) for i in range(per_iter)])

    def later(it, carry):
        base = it * (per_iter * QB)
        run([block(pl.multiple_of(base + i * QB, QB), False) for i in range(per_iter)])
        return carry

    lax.fori_loop(1, seq_len // (per_iter * QB), later, 0)


def _band_attn(proj, bias, *, batch, seq_len, q_col, k_col, v_col, width):
    G, H = LANE_GROUP, HEADS_PER_GROUP
    n_groups = width // G
    kern = functools.partial(_band_attn_kernel, seq_len=seq_len)

    def col(base):
        return pl.BlockSpec((seq_len, G), lambda b, g: (b, base // G + g))

    return pl.pallas_call(
        kern,
        grid=(batch, n_groups),
        in_specs=[
            col(q_col), col(k_col), col(v_col),
            pl.BlockSpec((H, Q_BLOCK, KEY_WINDOW), lambda b, g: (g, 0, 0)),
        ],
        out_specs=pl.BlockSpec((seq_len, G), lambda b, g: (b, g)),
        out_shape=jax.ShapeDtypeStruct((batch * seq_len, width), BF16),
        scratch_shapes=[
            pltpu.VMEM((seq_len + KEY_WINDOW - Q_BLOCK, G), BF16),
            pltpu.VMEM((seq_len + KEY_WINDOW - Q_BLOCK, G), BF16),
        ],
        compiler_params=_params("arbitrary", "arbitrary"),
        name="band_attention",
    )(proj, proj, proj, bias)


def _band_bias(rel_bias):
    heads, table = rel_bias.shape
    n_dist = KEY_WINDOW + Q_BLOCK - 1
    far = n_dist - (CHUNK + table)
    by_dist = jnp.concatenate([
        jnp.broadcast_to(rel_bias[:, :1], (heads, CHUNK)), rel_bias,
        jnp.broadcast_to(rel_bias[:, -1:], (heads, far))], axis=1).astype(F32)
    period = jnp.pad(by_dist[:, ::-1], ((0, 0), (0, 1)))
    skew = jnp.tile(period, (1, Q_BLOCK))[:, :Q_BLOCK * n_dist].reshape(heads, Q_BLOCK, n_dist)
    bias = skew[:, :, Q_BLOCK - 1:]
    qc = jnp.arange(Q_BLOCK)[:, None] // CHUNK
    kc = jnp.arange(KEY_WINDOW)[None, :] // CHUNK
    band = (kc >= qc) & (kc <= qc + LEFT_CHUNKS)
    return jnp.where(band[None], bias, MASK_VALUE)


def _merge_kernel(x_ref, ya_ref, yb_ref, za_ref, zb_ref, wa_ref, wb_ref, wo_ref, g_ref, o_ref):
    mixed = (_sigmoid(za_ref[...].astype(F32)) * _dot(ya_ref[...], wa_ref[...])
             + _sigmoid(zb_ref[...].astype(F32)) * _dot(yb_ref[...], wb_ref[...]))
    y = _dot(mixed.astype(BF16), wo_ref[...])
    o_ref[...] = x_ref[...] + _rms_norm(y, g_ref[...])


def _merge(x2d, ya, yb, proj, wa, wb, wo, gain, *, gate_col, bm):
    t, d = x2d.shape
    row = lambda c: pl.BlockSpec((bm, d), lambda m: (m, c))
    full = lambda a: pl.BlockSpec(a.shape, lambda m: (0, 0))
    return pl.pallas_call(
        _merge_kernel,
        grid=(t // bm,),
        in_specs=[row(0), row(0), row(0), row(gate_col // d), row(gate_col // d + 1),
                  full(wa), full(wb), full(wo), full(gain)],
        out_specs=row(0),
        out_shape=jax.ShapeDtypeStruct((t, d), F32),
        compiler_params=_params("arbitrary"),
        name="gated_merge",
    )(x2d, ya, yb, proj, proj, wa, wb, wo, gain)


def _cross_attn_kernel(x_ref, kv_ref, wq_ref, wo_ref, gpre_ref, gpost_ref, o_ref, *, n_heads):
    x = x_ref[...]
    h = _rms_norm(x, gpre_ref[...]).astype(BF16)
    q = _dot(h, wq_ref[...]).astype(BF16)
    width = q.shape[1]
    hd = width // n_heads
    scale = hd ** -0.5
    outs = []
    for i in range(n_heads):
        qh = q[:, i * hd:(i + 1) * hd]
        kh = kv_ref[:, i * hd:(i + 1) * hd]
        vh = kv_ref[:, width + i * hd:width + (i + 1) * hd]
        s = _dot_nt(qh, kh) * scale
        m = jnp.max(s, axis=-1, keepdims=True)
        p = jnp.exp(s - m)
        denom = jnp.sum(p, axis=-1, keepdims=True)
        outs.append((_dot(p.astype(BF16), vh) / denom).astype(BF16))
    o = jnp.concatenate(outs, axis=-1)
    y = _dot(o, wo_ref[...])
    o_ref[...] = x + _rms_norm(y, gpost_ref[...])


def _cross_attn(x2d, kv, wq, wo, gpre, gpost, *, seq_len, mem_tokens, n_heads, bm):
    t, d = x2d.shape
    tiles_per_seq = seq_len // bm
    full = lambda a: pl.BlockSpec(a.shape, lambda m: (0, 0))
    kern = functools.partial(_cross_attn_kernel, n_heads=n_heads)
    return pl.pallas_call(
        kern,
        grid=(t // bm,),
        in_specs=[
            pl.BlockSpec((bm, d), lambda m: (m, 0)),
            pl.BlockSpec((mem_tokens, kv.shape[1]), lambda m: (m // tiles_per_seq, 0)),
            full(wq), full(wo), full(gpre), full(gpost),
        ],
        out_specs=pl.BlockSpec((bm, d), lambda m: (m, 0)),
        out_shape=jax.ShapeDtypeStruct((t, d), F32),
        compiler_params=_params("arbitrary"),
        name="cross_attention",
    )(x2d, kv, wq, wo, gpre, gpost)


def _ffn_kernel(x_ref, wg_ref, wu_ref, wo_ref, gpre_ref, gpost_ref, o_ref):
    x = x_ref[...]
    h = _rms_norm(x, gpre_ref[...]).astype(BF16)
    acc = jnp.zeros(x.shape, F32)
    for j in range(wg_ref.shape[0]):
        gate = _dot(h, wg_ref[j])
        up = _dot(h, wu_ref[j])
        act = (gate * _sigmoid(gate) * up).astype(BF16)
        acc = acc + _dot(act, wo_ref[j])
    o_ref[...] = x + _rms_norm(acc, gpost_ref[...])


def _ffn(x2d, wg, wu, wo, gpre, gpost, *, bm):
    t, d = x2d.shape
    resident = lambda a: pl.BlockSpec(a.shape, lambda m: (0,) * a.ndim,
                                      pipeline_mode=pl.Buffered(1))
    return pl.pallas_call(
        _ffn_kernel,
        grid=(t // bm,),
        in_specs=[
            pl.BlockSpec((bm, d), lambda m: (m, 0)),
            resident(wg), resident(wu), resident(wo), resident(gpre), resident(gpost),
        ],
        out_specs=pl.BlockSpec((bm, d), lambda m: (m, 0)),
        out_shape=jax.ShapeDtypeStruct((t, d), F32),
        compiler_params=_params("arbitrary"),
        name="swiglu_ffn",
    )(x2d, wg, wu, wo, gpre, gpost)


def _pad_to(a, rows=None, cols=None):
    r = a.shape[0] if rows is None else rows
    c = a.shape[1] if cols is None else cols
    return jnp.pad(a, ((0, r - a.shape[0]), (0, c - a.shape[1])))


def _layer(x2d, mem2d, p, *, batch, seq_len):
    d = x2d.shape[1]
    width = p["decay_up"].shape[1]
    n_dec, n_icl, n_gate = p["decay_up"].shape[0], p["iclr_up"].shape[0], p["gate_up"].shape[0]
    gate_slot = -(-n_gate // LORA_SLOT) * LORA_SLOT
    rwkv_in = 3 * width + n_dec + n_icl + n_gate
    row = lambda v: v.reshape(1, -1).astype(F32)

    w_in, mix = p["w_in"], p["shift_mix"]
    lo = 3 * width
    lora_w = jnp.concatenate([
        _pad_to(w_in[:, lo:lo + n_dec], cols=LORA_SLOT),
        _pad_to(w_in[:, lo + n_dec:lo + n_dec + n_icl], cols=LORA_SLOT),
        _pad_to(w_in[:, lo + n_dec + n_icl:rwkv_in], cols=gate_slot)], axis=1)
    lora_mix = jnp.concatenate([
        jnp.pad(mix[lo:lo + n_dec], (0, LORA_SLOT - n_dec)),
        jnp.pad(mix[lo + n_dec:lo + n_dec + n_icl], (0, LORA_SLOT - n_icl)),
        jnp.pad(mix[lo + n_dec + n_icl:rwkv_in], (0, gate_slot - n_gate))])
    wdec = _pad_to(p["decay_up"], rows=LORA_SLOT).astype(BF16)
    wicl = _pad_to(p["iclr_up"], rows=LORA_SLOT).astype(BF16)
    wgate = _pad_to(p["gate_up"], rows=gate_slot).astype(BF16)
    vec = jnp.stack([p["decay_base"], p["iclr_base"], p["key_norm_scale"], p["key_iclr_scale"],
                     p["bonus_scale"].reshape(-1), p["lnx_w"], p["lnx_b"],
                     jnp.zeros((width,), F32)]).astype(F32)

    bm_proj = min(1024, seq_len)
    g_pre = row(p["g_pre_mix"])
    rkv = _norm_proj(x2d, g_pre, w_in[:, :lo].astype(BF16), row(mix[:lo]),
                     seq_len=seq_len, bm=bm_proj, bn=1024)
    zl = _norm_proj(x2d, g_pre, lora_w.astype(BF16), row(lora_mix),
                    seq_len=seq_len, bm=bm_proj, bn=lora_w.shape[1])
    proj = _norm_proj(x2d, g_pre, w_in[:, rwkv_in:].astype(BF16), None,
                      seq_len=seq_len, bm=bm_proj, bn=1024)
    n_seqs = 2 if batch % 2 == 0 else 1
    ya = _rwkv(rkv.reshape(batch, seq_len, -1), zl.reshape(batch, seq_len, -1),
               wdec, wicl, wgate, vec, batch=batch, seq_len=seq_len,
               r_col=0, k_col=width, v_col=2 * width, tb=min(512, seq_len), n_seqs=n_seqs)
    ya = ya.reshape(batch * seq_len, width)
    yb = _band_attn(proj, _band_bias(p["rel_bias"]), batch=batch, seq_len=seq_len,
                    q_col=0, k_col=width, v_col=2 * width, width=width)
    x2d = _merge(x2d, ya, yb, proj, p["w_branch_a"].astype(BF16), p["w_branch_b"].astype(BF16),
                 p["w_out"].astype(BF16), row(p["g_post_mix"]), gate_col=3 * width, bm=512)

    mem_tokens = mem2d.shape[0] // batch
    kv = _norm_proj(mem2d, row(p["g_mem"]), p["w_kv_mem"].astype(BF16), None,
                    seq_len=mem_tokens, bm=min(1024, mem2d.shape[0]),
                    bn=p["w_kv_mem"].shape[1])
    x2d = _cross_attn(x2d, kv, p["w_q_mem"].astype(BF16), p["w_o_mem"].astype(BF16),
                      row(p["g_pre_cross"]), row(p["g_post_cross"]),
                      seq_len=seq_len, mem_tokens=mem_tokens, n_heads=4, bm=512)

    hidden = p["w_ffn_out"].shape[0]
    n_hc = hidden // LANE_GROUP
    w_ffn_in = p["w_ffn_in"].astype(BF16)
    wg = w_ffn_in[:, :hidden].reshape(d, n_hc, LANE_GROUP).transpose(1, 0, 2)
    wu = w_ffn_in[:, hidden:].reshape(d, n_hc, LANE_GROUP).transpose(1, 0, 2)
    wo = p["w_ffn_out"].astype(BF16).reshape(n_hc, LANE_GROUP, d)
    return _ffn(x2d, wg, wu, wo, row(p["g_pre_ffn"]), row(p["g_post_ffn"]), bm=512)


def kernel(x, mem, g_pre_mix, g_post_mix, w_in, shift_mix, decay_base, decay_up, iclr_base, iclr_up, gate_up, key_norm_scale, key_iclr_scale, bonus_scale, lnx_w, lnx_b, rel_bias, w_branch_a, w_branch_b, w_out, g_pre_cross, g_post_cross, g_mem, w_q_mem, w_kv_mem, w_o_mem, g_pre_ffn, g_post_ffn, w_ffn_in, w_ffn_out):
    batch, seq_len, d = x.shape
    stacked = dict(
        g_pre_mix=g_pre_mix, g_post_mix=g_post_mix, w_in=w_in, shift_mix=shift_mix,
        decay_base=decay_base, decay_up=decay_up, iclr_base=iclr_base, iclr_up=iclr_up,
        gate_up=gate_up, key_norm_scale=key_norm_scale, key_iclr_scale=key_iclr_scale,
        bonus_scale=bonus_scale, lnx_w=lnx_w, lnx_b=lnx_b, rel_bias=rel_bias,
        w_branch_a=w_branch_a, w_branch_b=w_branch_b, w_out=w_out, g_pre_cross=g_pre_cross,
        g_post_cross=g_post_cross, g_mem=g_mem, w_q_mem=w_q_mem, w_kv_mem=w_kv_mem,
        w_o_mem=w_o_mem, g_pre_ffn=g_pre_ffn, g_post_ffn=g_post_ffn, w_ffn_in=w_ffn_in,
        w_ffn_out=w_ffn_out)
    x2d = x.reshape(batch * seq_len, d)
    mem2d = mem.reshape(-1, d)
    for layer in range(g_pre_mix.shape[0]):
        x2d = _layer(x2d, mem2d, {k: v[layer] for k, v in stacked.items()},
                     batch=batch, seq_len=seq_len)
    return x2d.reshape(batch, seq_len, d)
```

```python
import functools
import itertools

import jax
import jax.numpy as jnp
from jax import lax
from jax.experimental import pallas as pl
from jax.experimental.pallas import tpu as pltpu

F32 = jnp.float32
BF16 = jnp.bfloat16

NORM_EPS = 1e-6
GROUP_NORM_EPS = 64e-5
MASK_VALUE = -1e30
DECAY_LOG_SCALE = 0.6065306597126334

HEAD_DIM = 64
LANE_GROUP = 256
HEADS_PER_GROUP = LANE_GROUP // HEAD_DIM
CHUNK = 64
LEFT_CHUNKS = 8
REL_CLIP = 128
Q_BLOCK = 2 * CHUNK
KEY_WINDOW = LEFT_CHUNKS * CHUNK + Q_BLOCK
LORA_SLOT = 128
VMEM_LIMIT = 48 * 1024 * 1024


def _dot(a, b):
    return jnp.dot(a, b, preferred_element_type=F32)


def _dot_nt(a, b):
    return lax.dot_general(a, b, (((1,), (1,)), ((), ())), preferred_element_type=F32)


def _sigmoid(u):
    return 1.0 / (1.0 + jnp.exp(-u))


def _rms_norm(xf, gain):
    ms = jnp.mean(xf * xf, axis=-1, keepdims=True)
    return xf * lax.rsqrt(ms + NORM_EPS) * gain


def _params(*sem):
    return pltpu.CompilerParams(dimension_semantics=sem, vmem_limit_bytes=VMEM_LIMIT)


def _norm_proj_kernel(*refs, shifted, bn, seq_tiles):
    n_out = len(shifted)
    n_mix = sum(shifted)
    x_ref, g_ref = refs[:2]
    w_refs = refs[2:2 + n_out]
    mix_refs = iter(refs[2 + n_out:2 + n_out + n_mix])
    o_refs = refs[2 + n_out + n_mix:2 + 2 * n_out + n_mix]
    carry_refs = refs[2 + 2 * n_out + n_mix:]

    if n_mix:
        @pl.when(pl.program_id(0) % seq_tiles == 0)
        def _():
            for carry in carry_refs:
                carry[...] = jnp.zeros(carry.shape, F32)

    h = _rms_norm(x_ref[...], g_ref[...]).astype(BF16)
    bm = h.shape[0]
    carries = iter(carry_refs)
    for w_ref, o_ref, shift in zip(w_refs, o_refs, shifted):
        mix_ref, carry = (next(mix_refs), next(carries)) if shift else (None, None)
        step = min(bn, w_ref.shape[1])
        for n in range(w_ref.shape[1] // step):
            cols = slice(n * step, (n + 1) * step)
            p = _dot(h, w_ref[:, cols])
            if shift:
                row = lax.broadcasted_iota(jnp.int32, p.shape, 0)
                prev = jnp.where(row == 0, carry[7:8, cols], pltpu.roll(p, 1, 0))
                o_ref[:, cols] = (p + (prev - p) * mix_ref[:, cols]).astype(o_ref.dtype)
                carry[:, cols] = p[bm - 8:, :]
            else:
                o_ref[:, cols] = p.astype(o_ref.dtype)


def _norm_proj(x2d, gain, weights, mixes, *, seq_len, bm, bn):
    t, d = x2d.shape
    shifted = tuple(mix is not None for mix in mixes)
    mix_rows = [mix for mix in mixes if mix is not None]
    resident = lambda a: pl.BlockSpec(a.shape, lambda m: (0, 0), pipeline_mode=pl.Buffered(1))
    kern = functools.partial(_norm_proj_kernel, shifted=shifted, bn=bn,
                             seq_tiles=max(seq_len // bm, 1))
    return pl.pallas_call(
        kern,
        grid=(t // bm,),
        in_specs=[pl.BlockSpec((bm, d), lambda m: (m, 0)), resident(gain)]
        + [resident(w) for w in weights] + [resident(mix) for mix in mix_rows],
        out_specs=[pl.BlockSpec((bm, w.shape[1]), lambda m: (m, 0)) for w in weights],
        out_shape=[jax.ShapeDtypeStruct((t, w.shape[1]), BF16) for w in weights],
        scratch_shapes=[pltpu.VMEM((8, mix.shape[1]), F32) for mix in mix_rows],
        compiler_params=_params("arbitrary"),
        name="norm_proj",
    )(x2d, gain, *weights, *mix_rows)


def _rwkv_kernel(r_ref, k_ref, v_ref, zl_ref, wdec_ref, wicl_ref, wgate_ref, vec_ref,
                 o_ref, st_scr, *, n_chunks, n_groups, n_seqs):
    C, G, H = CHUNK, LANE_GROUP, HEADS_PER_GROUP

    tt = lax.broadcasted_iota(jnp.int32, (C, G), 0)
    ss = lax.broadcasted_iota(jnp.int32, (C, G), 1) & (C - 1)
    m_strict = ss < tt
    m_incl = ss <= tt
    eye_c = jnp.where(ss == tt, 1.0, 0.0).astype(F32)
    r0 = lax.broadcasted_iota(jnp.int32, (G, G), 0)
    c0 = lax.broadcasted_iota(jnp.int32, (G, G), 1)
    blk = (r0 >> 6) == (c0 >> 6)
    ones_bd = jnp.where(blk, 1.0, 0.0).astype(BF16)
    tr = lax.broadcasted_iota(jnp.int32, (C, 2 * C), 0)
    tc = lax.broadcasted_iota(jnp.int32, (C, 2 * C), 1) & (C - 1)
    tri2 = jnp.where(tc <= tr, 1.0, 0.0).astype(BF16)

    def bd(xc):
        return jnp.where(blk, jnp.concatenate([xc] * H, axis=0), 0.0).astype(BF16)

    def seg_sum(xc):
        return _dot(xc.astype(BF16), ones_bd)

    @pl.when(pl.program_id(1) == 0)
    def _():
        st_scr[...] = jnp.zeros(st_scr.shape, F32)

    def group_chunk(b, g, t0, u, a, gate):
        lanes = slice(g * G, (g + 1) * G)
        vec = vec_ref[:, lanes]
        kns, kis, bonus_scale = vec[2:3], vec[3:4], vec[4:5]
        lnx_w, lnx_b = vec[5:6], vec[6:7]
        rc = r_ref[b, pl.ds(t0, C), lanes].astype(F32)
        kc = k_ref[b, pl.ds(t0, C), lanes].astype(F32)
        vc = v_ref[b, pl.ds(t0, C), lanes].astype(F32)

        ld = -DECAY_LOG_SCALE * _sigmoid(u)
        ld_hi = ld.astype(BF16)
        ld_lo = (ld - ld_hi.astype(F32)).astype(BF16)
        cum = _dot(tri2, jnp.concatenate([ld_hi, ld_lo], axis=0))
        kk = kc * kns
        kk_ss = seg_sum(kk * kk)
        k2 = kc * (1.0 + (a - 1.0) * kis)
        bonus_s = seg_sum(rc * k2 * bonus_scale)
        yield

        kk = kk * lax.rsqrt(jnp.maximum(kk_ss, 1e-24))
        beta = kk * a
        bonus = bonus_s * vc
        p_t = jnp.exp(cum)
        p_inv = jnp.exp(-cum)
        p_prev = jnp.exp(cum - ld)
        p_last = p_t[C - 1:C, :]
        rt, kt, bt, kkt = rc * p_t, k2 * p_inv, beta * p_inv, kk * p_prev
        kh, bh = kt * p_last, bt * p_last

        lhs = jnp.concatenate([kkt, rt], axis=0).astype(BF16)
        ab = _dot_nt(lhs, bd(bt))
        ak = _dot_nt(lhs, bd(kt))
        st = st_scr[b, g]
        ws = _dot_nt(lhs, st.astype(BF16))
        yield
        a1 = jnp.where(m_strict, ab[:C], 0.0)
        a4 = jnp.where(m_incl, ab[C:], 0.0)
        a2 = jnp.where(m_strict, ak[:C], 0.0)
        a3 = jnp.where(m_incl, ak[C:], 0.0)

        x = -a1
        t_inv = eye_c + x
        xn = _dot(x.astype(BF16), bd(x))
        av = _dot(jnp.concatenate([a2, a3], axis=0).astype(BF16), bd(vc))
        yield
        rhs_u = ws[:C] + av[:C]
        y_part = ws[C:] + av[C:]
        for level in range(1, 6):
            xbd = bd(xn)
            if level < 5:
                both = _dot(jnp.concatenate([t_inv, xn], axis=0).astype(BF16), xbd)
                yield
                t_inv = t_inv + both[:C]
                xn = both[C:]
            else:
                t_inv = t_inv + _dot(t_inv.astype(BF16), xbd)
                yield

        u_mat = _dot(t_inv.astype(BF16), bd(rhs_u))
        yield
        y = y_part - _dot(a4.astype(BF16), bd(u_mat))
        vu_t = jnp.concatenate([vc, -u_mat], axis=0).T.astype(BF16)
        kb = jnp.concatenate([kh, bh], axis=0).astype(BF16)
        st_scr[b, g] = st * p_last + jnp.where(blk, _dot(vu_t, kb), 0.0)
        yield

        mu = seg_sum(y) * (1.0 / HEAD_DIM)
        yield
        d = y - mu
        var = seg_sum(d * d) * (1.0 / HEAD_DIM)
        yield
        yn = d * lax.rsqrt(var + GROUP_NORM_EPS) * lnx_w + lnx_b
        o_ref[b, pl.ds(t0, C), lanes] = ((yn + bonus) * gate(b, g)).astype(o_ref.dtype)

    def chunk(j, carry):
        t0 = pl.multiple_of(j * C, C)

        def lora_in(lo, hi):
            return jnp.concatenate(
                [zl_ref[b, pl.ds(t0, C), lo:hi] for b in range(n_seqs)], axis=0)

        zw = lora_in(0, LORA_SLOT).astype(F32)
        u = vec_ref[0:1, :] + _dot(jnp.tanh(zw).astype(BF16), wdec_ref[...])
        a = _sigmoid(vec_ref[1:2, :] + _dot(lora_in(LORA_SLOT, 2 * LORA_SLOT), wicl_ref[...]))

        gate_all = []

        def gate(b, g):
            if not gate_all:
                zg = lora_in(2 * LORA_SLOT, zl_ref.shape[2]).astype(F32)
                gate_all.append(_dot(_sigmoid(zg).astype(BF16), wgate_ref[...]))
            return gate_all[0][b * C:(b + 1) * C, g * G:(g + 1) * G]

        chains = []
        for b in range(n_seqs):
            rows = slice(b * C, (b + 1) * C)
            for g in range(n_groups):
                lanes = slice(g * G, (g + 1) * G)
                chains.append(group_chunk(b, g, t0, u[rows, lanes], a[rows, lanes], gate))
        for _ in itertools.zip_longest(*chains):
            pass
        return carry

    lax.fori_loop(0, n_chunks, chunk, 0)


def _rwkv(proj, zl, wdec, wicl, wgate, vec, *, batch, seq_len, r_col, k_col, v_col, tb, n_seqs):
    G = LANE_GROUP
    width = wdec.shape[1]
    n_groups = width // G
    kern = functools.partial(_rwkv_kernel, n_chunks=tb // CHUNK, n_groups=n_groups,
                             n_seqs=n_seqs)

    def col(base):
        return pl.BlockSpec((n_seqs, tb, width), lambda b, s: (b, s, base // width))

    full = lambda a: pl.BlockSpec(a.shape, lambda b, s: (0, 0))
    return pl.pallas_call(
        kern,
        grid=(batch // n_seqs, seq_len // tb),
        in_specs=[
            col(r_col), col(k_col), col(v_col),
            pl.BlockSpec((n_seqs, tb, zl.shape[2]), lambda b, s: (b, s, 0)),
            full(wdec), full(wicl), full(wgate), full(vec),
        ],
        out_specs=pl.BlockSpec((n_seqs, tb, width), lambda b, s: (b, s, 0)),
        out_shape=jax.ShapeDtypeStruct((batch, seq_len, width), BF16),
        scratch_shapes=[pltpu.VMEM((n_seqs, n_groups, G, G), F32)],
        compiler_params=_params("arbitrary", "arbitrary"),
        name="rwkv_time_mix",
    )(proj, proj, proj, zl, wdec, wicl, wgate, vec)


def _band_attn_kernel(q_ref, k_ref, v_ref, bias_ref, o_ref, kpad, vpad, *, seq_len):
    G, H, QB, KW = LANE_GROUP, HEADS_PER_GROUP, Q_BLOCK, KEY_WINDOW
    pad = KW - QB
    kpad[0:pad, :] = jnp.zeros((pad, G), BF16)
    vpad[0:pad, :] = jnp.zeros((pad, G), BF16)
    kpad[pad:, :] = k_ref[...]
    vpad[pad:, :] = v_ref[...]

    lane_head = lax.broadcasted_iota(jnp.int32, (QB, G), 1) >> 6
    key_idx = lax.broadcasted_iota(jnp.int32, (H * QB, KW), 1)
    scale = HEAD_DIM ** -0.5
    per_iter = pad // QB

    def block(q0, first_keys):
        qv = q_ref[pl.ds(q0, QB), :].astype(F32) * scale
        kw = kpad[pl.ds(q0, KW), :]
        vw = vpad[pl.ds(q0, KW), :]
        qs = jnp.concatenate(
            [jnp.where(lane_head == h, qv, 0.0) for h in range(H)], axis=0).astype(BF16)
        s = _dot_nt(qs, kw)
        yield
        s = s + bias_ref[...].reshape(H * QB, KW)
        if first_keys:
            s = jnp.where(key_idx >= pad - q0, s, MASK_VALUE)
        m = jnp.max(s, axis=-1, keepdims=True)
        p = jnp.exp(s - m)
        denom = jnp.sum(p, axis=-1, keepdims=True)
        o = _dot(p.astype(BF16), vw)
        yield
        o = o / denom
        out = jnp.where(lane_head == 0, o[0:QB], 0.0)
        for h in range(1, H):
            out = out + jnp.where(lane_head == h, o[h * QB:(h + 1) * QB], 0.0)
        o_ref[pl.ds(q0, QB), :] = out.astype(o_ref.dtype)

    def run(blocks):
        for _ in itertools.zip_longest(*blocks):
            pass

    run([block(i * QB, True) for i in range(per_iter)])

    def later(it, carry):
        base = it * (per_iter * QB)
        run([block(pl.multiple_of(base + i * QB, QB), False) for i in range(per_iter)])
        return carry

    lax.fori_loop(1, seq_len // (per_iter * QB), later, 0)


def _band_attn(proj, bias, *, batch, seq_len, q_col, k_col, v_col, width):
    G, H = LANE_GROUP, HEADS_PER_GROUP
    n_groups = width // G
    kern = functools.partial(_band_attn_kernel, seq_len=seq_len)

    def col(base):
        return pl.BlockSpec((seq_len, G), lambda b, g: (b, base // G + g))

    return pl.pallas_call(
        kern,
        grid=(batch, n_groups),
        in_specs=[
            col(q_col), col(k_col), col(v_col),
            pl.BlockSpec((H, Q_BLOCK, KEY_WINDOW), lambda b, g: (g, 0, 0)),
        ],
        out_specs=pl.BlockSpec((seq_len, G), lambda b, g: (b, g)),
        out_shape=jax.ShapeDtypeStruct((batch * seq_len, width), BF16),
        scratch_shapes=[
            pltpu.VMEM((seq_len + KEY_WINDOW - Q_BLOCK, G), BF16),
            pltpu.VMEM((seq_len + KEY_WINDOW - Q_BLOCK, G), BF16),
        ],
        compiler_params=_params("arbitrary", "arbitrary"),
        name="band_attention",
    )(proj, proj, proj, bias)


def _band_bias(rel_bias):
    heads, table = rel_bias.shape
    n_dist = KEY_WINDOW + Q_BLOCK - 1
    far = n_dist - (CHUNK + table)
    by_dist = jnp.concatenate([
        jnp.broadcast_to(rel_bias[:, :1], (heads, CHUNK)), rel_bias,
        jnp.broadcast_to(rel_bias[:, -1:], (heads, far))], axis=1).astype(F32)
    period = jnp.pad(by_dist[:, ::-1], ((0, 0), (0, 1)))
    skew = jnp.tile(period, (1, Q_BLOCK))[:, :Q_BLOCK * n_dist].reshape(heads, Q_BLOCK, n_dist)
    bias = skew[:, :, Q_BLOCK - 1:]
    qc = jnp.arange(Q_BLOCK)[:, None] // CHUNK
    kc = jnp.arange(KEY_WINDOW)[None, :] // CHUNK
    band = (kc >= qc) & (kc <= qc + LEFT_CHUNKS)
    return jnp.where(band[None], bias, MASK_VALUE)


def _merge_kernel(x_ref, ya_ref, yb_ref, za_ref, zb_ref, wa_ref, wb_ref, wo_ref, g_ref, o_ref):
    mixed = (_sigmoid(za_ref[...].astype(F32)) * _dot(ya_ref[...], wa_ref[...])
             + _sigmoid(zb_ref[...].astype(F32)) * _dot(yb_ref[...], wb_ref[...]))
    y = _dot(mixed.astype(BF16), wo_ref[...])
    o_ref[...] = x_ref[...] + _rms_norm(y, g_ref[...])


def _merge(x2d, ya, yb, proj, wa, wb, wo, gain, *, gate_col, bm):
    t, d = x2d.shape
    row = lambda c: pl.BlockSpec((bm, d), lambda m: (m, c))
    full = lambda a: pl.BlockSpec(a.shape, lambda m: (0, 0))
    return pl.pallas_call(
        _merge_kernel,
        grid=(t // bm,),
        in_specs=[row(0), row(0), row(0), row(gate_col // d), row(gate_col // d + 1),
                  full(wa), full(wb), full(wo), full(gain)],
        out_specs=row(0),
        out_shape=jax.ShapeDtypeStruct((t, d), F32),
        compiler_params=_params("arbitrary"),
        name="gated_merge",
    )(x2d, ya, yb, proj, proj, wa, wb, wo, gain)


def _cross_attn_kernel(x_ref, kv_ref, wq_ref, wo_ref, gpre_ref, gpost_ref, o_ref, *,
                       n_heads, n_sub):
    width = wq_ref.shape[1]
    hd = width // n_heads
    scale = hd ** -0.5
    sub_rows = x_ref.shape[0] // n_sub

    def sub_tile(i):
        rows = slice(i * sub_rows, (i + 1) * sub_rows)
        x = x_ref[rows, :]
        h = _rms_norm(x, gpre_ref[...]).astype(BF16)
        q = _dot(h, wq_ref[...]).astype(BF16)
        yield
        scores = [_dot_nt(q[:, j * hd:(j + 1) * hd], kv_ref[:, j * hd:(j + 1) * hd])
                  for j in range(n_heads)]
        yield
        outs = []
        for j, s in enumerate(scores):
            s = s * scale
            m = jnp.max(s, axis=-1, keepdims=True)
            p = jnp.exp(s - m)
            denom = jnp.sum(p, axis=-1, keepdims=True)
            vh = kv_ref[:, width + j * hd:width + (j + 1) * hd]
            outs.append((_dot(p.astype(BF16), vh), denom))
        yield
        o = jnp.concatenate([(pv / denom).astype(BF16) for pv, denom in outs], axis=-1)
        y = _dot(o, wo_ref[...])
        yield
        o_ref[rows, :] = x + _rms_norm(y, gpost_ref[...])

    for _ in itertools.zip_longest(*[sub_tile(i) for i in range(n_sub)]):
        pass


def _cross_attn(x2d, kv, wq, wo, gpre, gpost, *, seq_len, mem_tokens, n_heads, bm):
    t, d = x2d.shape
    tiles_per_seq = seq_len // bm
    full = lambda a: pl.BlockSpec(a.shape, lambda m: (0, 0))
    kern = functools.partial(_cross_attn_kernel, n_heads=n_heads, n_sub=2)
    return pl.pallas_call(
        kern,
        grid=(t // bm,),
        in_specs=[
            pl.BlockSpec((bm, d), lambda m: (m, 0)),
            pl.BlockSpec((mem_tokens, kv.shape[1]), lambda m: (m // tiles_per_seq, 0)),
            full(wq), full(wo), full(gpre), full(gpost),
        ],
        out_specs=pl.BlockSpec((bm, d), lambda m: (m, 0)),
        out_shape=jax.ShapeDtypeStruct((t, d), F32),
        compiler_params=_params("arbitrary"),
        name="cross_attention",
    )(x2d, kv, wq, wo, gpre, gpost)


def _ffn_kernel(x_ref, win_ref, wo_ref, gpre_ref, gpost_ref, o_ref):
    x = x_ref[...]
    h = _rms_norm(x, gpre_ref[...]).astype(BF16)
    hidden = wo_ref.shape[0]
    acc = jnp.zeros(x.shape, F32)
    for j in range(hidden // LANE_GROUP):
        cols = slice(j * LANE_GROUP, (j + 1) * LANE_GROUP)
        gate = _dot(h, win_ref[:, cols])
        up = _dot(h, win_ref[:, hidden + j * LANE_GROUP:hidden + (j + 1) * LANE_GROUP])
        act = (gate * _sigmoid(gate) * up).astype(BF16)
        acc = acc + _dot(act, wo_ref[cols, :])
    o_ref[...] = x + _rms_norm(acc, gpost_ref[...])


def _ffn(x2d, win, wo, gpre, gpost, *, bm):
    t, d = x2d.shape
    resident = lambda a: pl.BlockSpec(a.shape, lambda m: (0, 0), pipeline_mode=pl.Buffered(1))
    return pl.pallas_call(
        _ffn_kernel,
        grid=(t // bm,),
        in_specs=[
            pl.BlockSpec((bm, d), lambda m: (m, 0)),
            resident(win), resident(wo), resident(gpre), resident(gpost),
        ],
        out_specs=pl.BlockSpec((bm, d), lambda m: (m, 0)),
        out_shape=jax.ShapeDtypeStruct((t, d), F32),
        compiler_params=_params("arbitrary"),
        name="swiglu_ffn",
    )(x2d, win, wo, gpre, gpost)


def _pad_to(a, rows=None, cols=None):
    r = a.shape[0] if rows is None else rows
    c = a.shape[1] if cols is None else cols
    return jnp.pad(a, ((0, r - a.shape[0]), (0, c - a.shape[1])))


def _layer(x2d, mem2d, p, *, batch, seq_len):
    d = x2d.shape[1]
    width = p["decay_up"].shape[1]
    n_dec, n_icl, n_gate = p["decay_up"].shape[0], p["iclr_up"].shape[0], p["gate_up"].shape[0]
    gate_slot = -(-n_gate // LORA_SLOT) * LORA_SLOT
    rwkv_in = 3 * width + n_dec + n_icl + n_gate
    row = lambda v: v.reshape(1, -1).astype(F32)

    w_in, mix = p["w_in"], p["shift_mix"]
    lo = 3 * width
    lora_w = jnp.concatenate([
        _pad_to(w_in[:, lo:lo + n_dec], cols=LORA_SLOT),
        _pad_to(w_in[:, lo + n_dec:lo + n_dec + n_icl], cols=LORA_SLOT),
        _pad_to(w_in[:, lo + n_dec + n_icl:rwkv_in], cols=gate_slot)], axis=1)
    lora_mix = jnp.concatenate([
        jnp.pad(mix[lo:lo + n_dec], (0, LORA_SLOT - n_dec)),
        jnp.pad(mix[lo + n_dec:lo + n_dec + n_icl], (0, LORA_SLOT - n_icl)),
        jnp.pad(mix[lo + n_dec + n_icl:rwkv_in], (0, gate_slot - n_gate))])
    wdec = _pad_to(p["decay_up"], rows=LORA_SLOT).astype(BF16)
    wicl = _pad_to(p["iclr_up"], rows=LORA_SLOT).astype(BF16)
    wgate = _pad_to(p["gate_up"], rows=gate_slot).astype(BF16)
    vec = jnp.stack([p["decay_base"], p["iclr_base"], p["key_norm_scale"], p["key_iclr_scale"],
                     p["bonus_scale"].reshape(-1), p["lnx_w"], p["lnx_b"],
                     jnp.zeros((width,), F32)]).astype(F32)

    rkv, zl, proj = _norm_proj(
        x2d, row(p["g_pre_mix"]),
        [w_in[:, :lo].astype(BF16), lora_w.astype(BF16), w_in[:, rwkv_in:].astype(BF16)],
        [row(mix[:lo]), row(lora_mix), None], seq_len=seq_len, bm=min(512, seq_len), bn=1024)
    n_seqs = 2 if batch % 2 == 0 else 1
    ya = _rwkv(rkv.reshape(batch, seq_len, -1), zl.reshape(batch, seq_len, -1),
               wdec, wicl, wgate, vec, batch=batch, seq_len=seq_len,
               r_col=0, k_col=width, v_col=2 * width, tb=min(512, seq_len), n_seqs=n_seqs)
    ya = ya.reshape(batch * seq_len, width)
    yb = _band_attn(proj, _band_bias(p["rel_bias"]), batch=batch, seq_len=seq_len,
                    q_col=0, k_col=width, v_col=2 * width, width=width)
    x2d = _merge(x2d, ya, yb, proj, p["w_branch_a"].astype(BF16), p["w_branch_b"].astype(BF16),
                 p["w_out"].astype(BF16), row(p["g_post_mix"]), gate_col=3 * width, bm=512)

    mem_tokens = mem2d.shape[0] // batch
    kv, = _norm_proj(mem2d, row(p["g_mem"]), [p["w_kv_mem"].astype(BF16)], [None],
                     seq_len=mem_tokens, bm=min(512, mem2d.shape[0]), bn=1024)
    x2d = _cross_attn(x2d, kv, p["w_q_mem"].astype(BF16), p["w_o_mem"].astype(BF16),
                      row(p["g_pre_cross"]), row(p["g_post_cross"]),
                      seq_len=seq_len, mem_tokens=mem_tokens, n_heads=4, bm=512)

    return _ffn(x2d, p["w_ffn_in"].astype(BF16), p["w_ffn_out"].astype(BF16),
                row(p["g_pre_ffn"]), row(p["g_post_ffn"]), bm=512)


def kernel(x, mem, g_pre_mix, g_post_mix, w_in, shift_mix, decay_base, decay_up, iclr_base, iclr_up, gate_up, key_norm_scale, key_iclr_scale, bonus_scale, lnx_w, lnx_b, rel_bias, w_branch_a, w_branch_b, w_out, g_pre_cross, g_post_cross, g_mem, w_q_mem, w_kv_mem, w_o_mem, g_pre_ffn, g_post_ffn, w_ffn_in, w_ffn_out):
    batch, seq_len, d = x.shape
    stacked = dict(
        g_pre_mix=g_pre_mix, g_post_mix=g_post_mix, w_in=w_in, shift_mix=shift_mix,
        decay_base=decay_base, decay_up=decay_up, iclr_base=iclr_base, iclr_up=iclr_up,
        gate_up=gate_up, key_norm_scale=key_norm_scale, key_iclr_scale=key_iclr_scale,
        bonus_scale=bonus_scale, lnx_w=lnx_w, lnx_b=lnx_b, rel_bias=rel_bias,
        w_branch_a=w_branch_a, w_branch_b=w_branch_b, w_out=w_out, g_pre_cross=g_pre_cross,
        g_post_cross=g_post_cross, g_mem=g_mem, w_q_mem=w_q_mem, w_kv_mem=w_kv_mem,
        w_o_mem=w_o_mem, g_pre_ffn=g_pre_ffn, g_post_ffn=g_post_ffn, w_ffn_in=w_ffn_in,
        w_ffn_out=w_ffn_out)
    x2d = x.reshape(batch * seq_len, d)
    mem2d = mem.reshape(-1, d)
    for layer in range(g_pre_mix.shape[0]):
        x2d = _layer(x2d, mem2d, {k: v[layer] for k, v in stacked.items()},
                     batch=batch, seq_len=seq_len)
    return x2d.reshape(batch, seq_len, d)
```

```python
import functools

import jax
import jax.numpy as jnp
from jax import lax
from jax.experimental import pallas as pl
from jax.experimental.pallas import tpu as pltpu

F32 = jnp.float32
BF16 = jnp.bfloat16

NORM_EPS = 1e-6
GROUP_NORM_EPS = 64e-5
MASK_VALUE = -1e30
DECAY_LOG_SCALE = 0.6065306597126334
LOG2_E = 1.4426950408889634

HEAD_DIM = 64
LANE_GROUP = 256
HEADS_PER_GROUP = LANE_GROUP // HEAD_DIM
CHUNK = 64
LEFT_CHUNKS = 8
REL_CLIP = 128
Q_BLOCK = 2 * CHUNK
KEY_WINDOW = LEFT_CHUNKS * CHUNK + Q_BLOCK
LORA_SLOT = 128
VMEM_LIMIT = 48 * 1024 * 1024


def _dot(a, b):
    return jnp.dot(a, b, preferred_element_type=F32)


def _dot_nt(a, b):
    return lax.dot_general(a, b, (((1,), (1,)), ((), ())), preferred_element_type=F32)


def _sigmoid(u):
    return 1.0 / (1.0 + jnp.exp(-u))


def _rms_norm(xf, gain):
    ms = jnp.mean(xf * xf, axis=-1, keepdims=True)
    return xf * lax.rsqrt(ms + NORM_EPS) * gain


def _params(*sem):
    return pltpu.CompilerParams(dimension_semantics=sem, vmem_limit_bytes=VMEM_LIMIT)


def _run_pipelined(stages, batch=1):
    pending, running = list(stages), []
    while pending or running:
        running.extend(pending[:batch])
        del pending[:batch]
        alive = []
        for gen in running:
            if next(gen, StopIteration) is not StopIteration:
                alive.append(gen)
        running = alive


def _norm_proj_kernel(*refs, shifted, bn, seq_tiles):
    n_out = len(shifted)
    n_mix = sum(shifted)
    x_ref, g_ref = refs[:2]
    w_refs = refs[2:2 + n_out]
    mix_refs = iter(refs[2 + n_out:2 + n_out + n_mix])
    o_refs = refs[2 + n_out + n_mix:2 + 2 * n_out + n_mix]
    carry_refs = refs[2 + 2 * n_out + n_mix:]

    if n_mix:
        @pl.when(pl.program_id(0) % seq_tiles == 0)
        def _():
            for carry in carry_refs:
                carry[...] = jnp.zeros(carry.shape, F32)

    h = _rms_norm(x_ref[...], g_ref[...]).astype(BF16)
    bm = h.shape[0]
    carries = iter(carry_refs)
    for w_ref, o_ref, shift in zip(w_refs, o_refs, shifted):
        mix_ref, carry = (next(mix_refs), next(carries)) if shift else (None, None)
        step = min(bn, w_ref.shape[1])
        for n in range(w_ref.shape[1] // step):
            cols = slice(n * step, (n + 1) * step)
            p = _dot(h, w_ref[:, cols])
            if shift:
                row = lax.broadcasted_iota(jnp.int32, p.shape, 0)
                prev = jnp.where(row == 0, carry[7:8, cols], pltpu.roll(p, 1, 0))
                o_ref[:, cols] = (p + (prev - p) * mix_ref[:, cols]).astype(o_ref.dtype)
                carry[:, cols] = p[bm - 8:, :]
            else:
                o_ref[:, cols] = p.astype(o_ref.dtype)


def _norm_proj(x2d, gain, weights, mixes, *, seq_len, bm, bn):
    t, d = x2d.shape
    shifted = tuple(mix is not None for mix in mixes)
    mix_rows = [mix for mix in mixes if mix is not None]
    resident = lambda a: pl.BlockSpec(a.shape, lambda m: (0, 0), pipeline_mode=pl.Buffered(1))
    kern = functools.partial(_norm_proj_kernel, shifted=shifted, bn=bn,
                             seq_tiles=max(seq_len // bm, 1))
    return pl.pallas_call(
        kern,
        grid=(t // bm,),
        in_specs=[pl.BlockSpec((bm, d), lambda m: (m, 0)), resident(gain)]
        + [resident(w) for w in weights] + [resident(mix) for mix in mix_rows],
        out_specs=[pl.BlockSpec((bm, w.shape[1]), lambda m: (m, 0)) for w in weights],
        out_shape=[jax.ShapeDtypeStruct((t, w.shape[1]), BF16) for w in weights],
        scratch_shapes=[pltpu.VMEM((8, mix.shape[1]), F32) for mix in mix_rows],
        compiler_params=_params("arbitrary"),
        name="norm_proj",
    )(x2d, gain, *weights, *mix_rows)


def _rwkv_kernel(r_ref, k_ref, v_ref, zl_ref, wdec_ref, wicl_ref, wgate_ref, vec_ref,
                 o_ref, st_scr, *, n_chunks, n_groups, n_seqs):
    C, G, H = CHUNK, LANE_GROUP, HEADS_PER_GROUP

    tt = lax.broadcasted_iota(jnp.int32, (C, G), 0)
    ss = lax.broadcasted_iota(jnp.int32, (C, G), 1) & (C - 1)
    m_strict = ss < tt
    m_incl = ss <= tt
    eye_c = jnp.where(ss == tt, 1.0, 0.0).astype(F32)
    r0 = lax.broadcasted_iota(jnp.int32, (G, G), 0)
    c0 = lax.broadcasted_iota(jnp.int32, (G, G), 1)
    blk = (r0 >> 6) == (c0 >> 6)
    ones_bd = jnp.where(blk, 1.0, 0.0).astype(BF16)
    tr = lax.broadcasted_iota(jnp.int32, (C, 2 * C), 0)
    tc = lax.broadcasted_iota(jnp.int32, (C, 2 * C), 1) & (C - 1)
    tri2 = jnp.where(tc <= tr, 1.0, 0.0).astype(BF16)

    def bd(xc):
        return jnp.where(blk, jnp.concatenate([xc] * H, axis=0), 0.0).astype(BF16)

    def seg_sum(xc):
        return _dot(xc.astype(BF16), ones_bd)

    @pl.when(pl.program_id(1) == 0)
    def _():
        st_scr[...] = jnp.zeros(st_scr.shape, F32)

    def group_chunk(b, g, t0, u, a, gate):
        lanes = slice(g * G, (g + 1) * G)
        vec = vec_ref[:, lanes]
        kns, kis, bonus_scale = vec[2:3], vec[3:4], vec[4:5]
        lnx_w, lnx_b = vec[5:6], vec[6:7]
        rc = r_ref[b, pl.ds(t0, C), lanes].astype(F32)
        kc = k_ref[b, pl.ds(t0, C), lanes].astype(F32)
        vc = v_ref[b, pl.ds(t0, C), lanes].astype(F32)

        ld = -DECAY_LOG_SCALE * _sigmoid(u)
        ld_hi = ld.astype(BF16)
        ld_lo = (ld - ld_hi.astype(F32)).astype(BF16)
        cum = _dot(tri2, jnp.concatenate([ld_hi, ld_lo], axis=0))
        kk = kc * kns
        kk_ss = seg_sum(kk * kk)
        k2 = kc * (1.0 + (a - 1.0) * kis)
        bonus_s = seg_sum(rc * k2 * bonus_scale)
        yield

        kk = kk * lax.rsqrt(jnp.maximum(kk_ss, 1e-24))
        beta = kk * a
        p_t = jnp.exp(cum)
        p_inv = jnp.exp(-cum)
        p_prev = jnp.exp(cum - ld)
        p_last = p_t[C - 1:C, :]
        rt, kt, bt, kkt = rc * p_t, k2 * p_inv, beta * p_inv, kk * p_prev

        lhs = jnp.concatenate([kkt, rt], axis=0).astype(BF16)
        ab = _dot_nt(lhs, bd(bt))
        ak = _dot_nt(lhs, bd(kt))
        ws = _dot_nt(lhs, st_scr[b, g].astype(BF16))
        yield
        a1 = jnp.where(m_strict, ab[:C], 0.0)
        a4 = jnp.where(m_incl, ab[C:], 0.0)
        a2 = jnp.where(m_strict, ak[:C], 0.0)
        a3 = jnp.where(m_incl, ak[C:], 0.0)

        x = -a1
        t_inv = eye_c + x
        xn = _dot(x.astype(BF16), bd(x))
        av = _dot(jnp.concatenate([a2, a3], axis=0).astype(BF16), bd(vc))
        yield
        rhs_u = ws[:C] + av[:C]
        y_part = ws[C:] + av[C:]
        for level in range(1, 6):
            xbd = bd(xn)
            if level < 5:
                both = _dot(jnp.concatenate([t_inv, xn], axis=0).astype(BF16), xbd)
                yield
                t_inv = t_inv + both[:C]
                xn = both[C:]
            else:
                t_inv = t_inv + _dot(t_inv.astype(BF16), xbd)
                yield

        u_mat = _dot(t_inv.astype(BF16), bd(rhs_u))
        yield
        y = y_part - _dot(a4.astype(BF16), bd(u_mat))
        vu_t = jnp.concatenate([vc, -u_mat], axis=0).T.astype(BF16)
        kb = jnp.concatenate([kt * p_last, bt * p_last], axis=0).astype(BF16)
        st_scr[b, g] = st_scr[b, g] * p_last + jnp.where(blk, _dot(vu_t, kb), 0.0)
        yield

        mu = seg_sum(y) * (1.0 / HEAD_DIM)
        yield
        d = y - mu
        var = seg_sum(d * d) * (1.0 / HEAD_DIM)
        yield
        yn = d * lax.rsqrt(var + GROUP_NORM_EPS) * lnx_w + lnx_b
        out = (yn + bonus_s * vc) * gate(b, g)
        o_ref[b, pl.ds(t0, C), lanes] = out.astype(o_ref.dtype)

    def chunk(j, carry):
        t0 = pl.multiple_of(j * C, C)

        def lora_in(lo, hi):
            return jnp.concatenate(
                [zl_ref[b, pl.ds(t0, C), lo:hi] for b in range(n_seqs)], axis=0)

        zw = lora_in(0, LORA_SLOT).astype(F32)
        u = vec_ref[0:1, :] + _dot(jnp.tanh(zw).astype(BF16), wdec_ref[...])
        a = _sigmoid(vec_ref[1:2, :] + _dot(lora_in(LORA_SLOT, 2 * LORA_SLOT), wicl_ref[...]))

        gate_all = []

        def gate(b, g):
            if not gate_all:
                zg = lora_in(2 * LORA_SLOT, zl_ref.shape[2]).astype(F32)
                gate_all.append(_dot(_sigmoid(zg).astype(BF16), wgate_ref[...]))
            return gate_all[0][b * C:(b + 1) * C, g * G:(g + 1) * G]

        chains = []
        for b in range(n_seqs):
            rows = slice(b * C, (b + 1) * C)
            for g in range(n_groups):
                lanes = slice(g * G, (g + 1) * G)
                chains.append(group_chunk(b, g, t0, u[rows, lanes], a[rows, lanes], gate))
        _run_pipelined(chains, batch=len(chains))
        return carry

    lax.fori_loop(0, n_chunks, chunk, 0)


def _rwkv(proj, zl, wdec, wicl, wgate, vec, *, batch, seq_len, r_col, k_col, v_col, tb, n_seqs):
    G = LANE_GROUP
    width = wdec.shape[1]
    n_groups = width // G
    kern = functools.partial(_rwkv_kernel, n_chunks=tb // CHUNK, n_groups=n_groups,
                             n_seqs=n_seqs)

    def col(base):
        return pl.BlockSpec((n_seqs, tb, width), lambda b, s: (b, s, base // width))

    full = lambda a: pl.BlockSpec(a.shape, lambda b, s: (0, 0))
    return pl.pallas_call(
        kern,
        grid=(batch // n_seqs, seq_len // tb),
        in_specs=[
            col(r_col), col(k_col), col(v_col),
            pl.BlockSpec((n_seqs, tb, zl.shape[2]), lambda b, s: (b, s, 0)),
            full(wdec), full(wicl), full(wgate), full(vec),
        ],
        out_specs=pl.BlockSpec((n_seqs, tb, width), lambda b, s: (b, s, 0)),
        out_shape=jax.ShapeDtypeStruct((batch, seq_len, width), BF16),
        scratch_shapes=[pltpu.VMEM((n_seqs, n_groups, G, G), F32)],
        compiler_params=_params("arbitrary", "arbitrary"),
        name="rwkv_time_mix",
    )(proj, proj, proj, zl, wdec, wicl, wgate, vec)


def _band_attn_kernel(q_ref, k_ref, v_ref, bias_ref, o_ref, *, seq_len):
    G, H, QB, KW = LANE_GROUP, HEADS_PER_GROUP, Q_BLOCK, KEY_WINDOW
    lane_head = lax.broadcasted_iota(jnp.int32, (QB, G), 1) >> 6
    scale = HEAD_DIM ** -0.5 * LOG2_E

    def block(q0):
        lo = max(q0 + QB - KW, 0)
        n_keys = q0 + QB - lo
        qv = q_ref[q0:q0 + QB, :].astype(F32) * scale
        qs = jnp.concatenate(
            [jnp.where(lane_head == h, qv, 0.0) for h in range(H)], axis=0).astype(BF16)
        s = _dot_nt(qs, k_ref[lo:lo + n_keys, :])
        yield True
        s = s + bias_ref[:, :, KW - n_keys:].reshape(H * QB, n_keys)
        m = jnp.max(s, axis=-1, keepdims=True)
        p = jnp.exp2(s - m)
        denom = jnp.sum(p, axis=-1, keepdims=True)
        o = _dot(p.astype(BF16), v_ref[lo:lo + n_keys, :])
        yield True
        o = o / denom
        out = jnp.where(lane_head == 0, o[0:QB], 0.0)
        for h in range(1, H):
            out = out + jnp.where(lane_head == h, o[h * QB:(h + 1) * QB], 0.0)
        o_ref[q0:q0 + QB, :] = out.astype(o_ref.dtype)

    _run_pipelined([block(q0) for q0 in range(0, seq_len, QB)])


def _band_attn(proj, bias, *, batch, seq_len, q_col, k_col, v_col, width):
    G, H = LANE_GROUP, HEADS_PER_GROUP
    n_groups = width // G
    kern = functools.partial(_band_attn_kernel, seq_len=seq_len)

    def col(base):
        return pl.BlockSpec((seq_len, G), lambda b, g: (b, base // G + g))

    return pl.pallas_call(
        kern,
        grid=(batch, n_groups),
        in_specs=[
            col(q_col), col(k_col), col(v_col),
            pl.BlockSpec((H, Q_BLOCK, KEY_WINDOW), lambda b, g: (g, 0, 0)),
        ],
        out_specs=pl.BlockSpec((seq_len, G), lambda b, g: (b, g)),
        out_shape=jax.ShapeDtypeStruct((batch * seq_len, width), BF16),
        compiler_params=_params("arbitrary", "arbitrary"),
        name="band_attention",
    )(proj, proj, proj, bias)


def _band_bias(rel_bias):
    heads, table = rel_bias.shape
    n_dist = KEY_WINDOW + Q_BLOCK - 1
    far = n_dist - (CHUNK + table)
    by_dist = jnp.concatenate([
        jnp.broadcast_to(rel_bias[:, :1], (heads, CHUNK)), rel_bias,
        jnp.broadcast_to(rel_bias[:, -1:], (heads, far))], axis=1).astype(F32)
    period = jnp.pad(by_dist[:, ::-1], ((0, 0), (0, 1)))
    skew = jnp.tile(period, (1, Q_BLOCK))[:, :Q_BLOCK * n_dist].reshape(heads, Q_BLOCK, n_dist)
    bias = skew[:, :, Q_BLOCK - 1:]
    qc = jnp.arange(Q_BLOCK)[:, None] // CHUNK
    kc = jnp.arange(KEY_WINDOW)[None, :] // CHUNK
    band = (kc >= qc) & (kc <= qc + LEFT_CHUNKS)
    return jnp.where(band[None], bias * LOG2_E, MASK_VALUE)


def _merge_kernel(x_ref, ya_ref, yb_ref, za_ref, zb_ref, wa_ref, wb_ref, wo_ref, g_ref, o_ref):
    mixed = (_sigmoid(za_ref[...].astype(F32)) * _dot(ya_ref[...], wa_ref[...])
             + _sigmoid(zb_ref[...].astype(F32)) * _dot(yb_ref[...], wb_ref[...]))
    y = _dot(mixed.astype(BF16), wo_ref[...])
    o_ref[...] = x_ref[...] + _rms_norm(y, g_ref[...])


def _merge(x2d, ya, yb, proj, wa, wb, wo, gain, *, gate_col, bm):
    t, d = x2d.shape
    row = lambda c: pl.BlockSpec((bm, d), lambda m: (m, c))
    full = lambda a: pl.BlockSpec(a.shape, lambda m: (0, 0))
    return pl.pallas_call(
        _merge_kernel,
        grid=(t // bm,),
        in_specs=[row(0), row(0), row(0), row(gate_col // d), row(gate_col // d + 1),
                  full(wa), full(wb), full(wo), full(gain)],
        out_specs=row(0),
        out_shape=jax.ShapeDtypeStruct((t, d), F32),
        compiler_params=_params("arbitrary"),
        name="gated_merge",
    )(x2d, ya, yb, proj, proj, wa, wb, wo, gain)


def _cross_attn_kernel(x_ref, kv_ref, wq_ref, wo_ref, gpre_ref, gpost_ref, o_ref, *,
                       n_heads, n_sub):
    width = wq_ref.shape[1]
    hd = width // n_heads
    scale = hd ** -0.5
    sub_rows = x_ref.shape[0] // n_sub

    def sub_tile(i):
        rows = slice(i * sub_rows, (i + 1) * sub_rows)
        x = x_ref[rows, :]
        h = _rms_norm(x, gpre_ref[...]).astype(BF16)
        q = _dot(h, wq_ref[...]).astype(BF16)
        yield
        scores = [_dot_nt(q[:, j * hd:(j + 1) * hd], kv_ref[:, j * hd:(j + 1) * hd])
                  for j in range(n_heads)]
        yield
        outs = []
        for j, s in enumerate(scores):
            s = s * scale
            m = jnp.max(s, axis=-1, keepdims=True)
            p = jnp.exp(s - m)
            denom = jnp.sum(p, axis=-1, keepdims=True)
            vh = kv_ref[:, width + j * hd:width + (j + 1) * hd]
            outs.append((_dot(p.astype(BF16), vh), denom))
        yield
        o = jnp.concatenate([(pv / denom).astype(BF16) for pv, denom in outs], axis=-1)
        y = _dot(o, wo_ref[...])
        yield
        o_ref[rows, :] = x + _rms_norm(y, gpost_ref[...])

    _run_pipelined([sub_tile(i) for i in range(n_sub)], batch=n_sub)


def _cross_attn(x2d, kv, wq, wo, gpre, gpost, *, seq_len, mem_tokens, n_heads, bm):
    t, d = x2d.shape
    tiles_per_seq = seq_len // bm
    full = lambda a: pl.BlockSpec(a.shape, lambda m: (0, 0))
    kern = functools.partial(_cross_attn_kernel, n_heads=n_heads, n_sub=2)
    return pl.pallas_call(
        kern,
        grid=(t // bm,),
        in_specs=[
            pl.BlockSpec((bm, d), lambda m: (m, 0)),
            pl.BlockSpec((mem_tokens, kv.shape[1]), lambda m: (m // tiles_per_seq, 0)),
            full(wq), full(wo), full(gpre), full(gpost),
        ],
        out_specs=pl.BlockSpec((bm, d), lambda m: (m, 0)),
        out_shape=jax.ShapeDtypeStruct((t, d), F32),
        compiler_params=_params("arbitrary"),
        name="cross_attention",
    )(x2d, kv, wq, wo, gpre, gpost)


def _ffn_kernel(x_ref, win_ref, wo_ref, gpre_ref, gpost_ref, o_ref):
    x = x_ref[...]
    h = _rms_norm(x, gpre_ref[...]).astype(BF16)
    hidden = wo_ref.shape[0]
    acc = jnp.zeros(x.shape, F32)
    for j in range(hidden // LANE_GROUP):
        cols = slice(j * LANE_GROUP, (j + 1) * LANE_GROUP)
        gate = _dot(h, win_ref[:, cols])
        up = _dot(h, win_ref[:, hidden + j * LANE_GROUP:hidden + (j + 1) * LANE_GROUP])
        act = (gate * _sigmoid(gate) * up).astype(BF16)
        acc = acc + _dot(act, wo_ref[cols, :])
    o_ref[...] = x + _rms_norm(acc, gpost_ref[...])


def _ffn(x2d, win, wo, gpre, gpost, *, bm):
    t, d = x2d.shape
    resident = lambda a: pl.BlockSpec(a.shape, lambda m: (0, 0), pipeline_mode=pl.Buffered(1))
    return pl.pallas_call(
        _ffn_kernel,
        grid=(t // bm,),
        in_specs=[
            pl.BlockSpec((bm, d), lambda m: (m, 0)),
            resident(win), resident(wo), resident(gpre), resident(gpost),
        ],
        out_specs=pl.BlockSpec((bm, d), lambda m: (m, 0)),
        out_shape=jax.ShapeDtypeStruct((t, d), F32),
        compiler_params=_params("arbitrary"),
        name="swiglu_ffn",
    )(x2d, win, wo, gpre, gpost)


def _pad_to(a, rows=None, cols=None):
    r = a.shape[0] if rows is None else rows
    c = a.shape[1] if cols is None else cols
    return jnp.pad(a, ((0, r - a.shape[0]), (0, c - a.shape[1])))


def _layer(x2d, mem2d, p, *, batch, seq_len):
    d = x2d.shape[1]
    width = p["decay_up"].shape[1]
    n_dec, n_icl, n_gate = p["decay_up"].shape[0], p["iclr_up"].shape[0], p["gate_up"].shape[0]
    gate_slot = -(-n_gate // LORA_SLOT) * LORA_SLOT
    rwkv_in = 3 * width + n_dec + n_icl + n_gate
    row = lambda v: v.reshape(1, -1).astype(F32)

    w_in, mix = p["w_in"], p["shift_mix"]
    lo = 3 * width
    lora_w = jnp.concatenate([
        _pad_to(w_in[:, lo:lo + n_dec], cols=LORA_SLOT),
        _pad_to(w_in[:, lo + n_dec:lo + n_dec + n_icl], cols=LORA_SLOT),
        _pad_to(w_in[:, lo + n_dec + n_icl:rwkv_in], cols=gate_slot)], axis=1)
    lora_mix = jnp.concatenate([
        jnp.pad(mix[lo:lo + n_dec], (0, LORA_SLOT - n_dec)),
        jnp.pad(mix[lo + n_dec:lo + n_dec + n_icl], (0, LORA_SLOT - n_icl)),
        jnp.pad(mix[lo + n_dec + n_icl:rwkv_in], (0, gate_slot - n_gate))])
    wdec = _pad_to(p["decay_up"], rows=LORA_SLOT).astype(BF16)
    wicl = _pad_to(p["iclr_up"], rows=LORA_SLOT).astype(BF16)
    wgate = _pad_to(p["gate_up"], rows=gate_slot).astype(BF16)
    vec = jnp.stack([p["decay_base"], p["iclr_base"], p["key_norm_scale"], p["key_iclr_scale"],
                     p["bonus_scale"].reshape(-1), p["lnx_w"], p["lnx_b"],
                     jnp.zeros((width,), F32)]).astype(F32)

    rkv, zl, proj = _norm_proj(
        x2d, row(p["g_pre_mix"]),
        [w_in[:, :lo].astype(BF16), lora_w.astype(BF16), w_in[:, rwkv_in:].astype(BF16)],
        [row(mix[:lo]), row(lora_mix), None], seq_len=seq_len, bm=min(512, seq_len), bn=1024)
    n_seqs = 2 if batch % 2 == 0 else 1
    ya = _rwkv(rkv.reshape(batch, seq_len, -1), zl.reshape(batch, seq_len, -1),
               wdec, wicl, wgate, vec, batch=batch, seq_len=seq_len,
               r_col=0, k_col=width, v_col=2 * width, tb=min(512, seq_len), n_seqs=n_seqs)
    ya = ya.reshape(batch * seq_len, width)
    yb = _band_attn(proj, _band_bias(p["rel_bias"]), batch=batch, seq_len=seq_len,
                    q_col=0, k_col=width, v_col=2 * width, width=width)
    x2d = _merge(x2d, ya, yb, proj, p["w_branch_a"].astype(BF16), p["w_branch_b"].astype(BF16),
                 p["w_out"].astype(BF16), row(p["g_post_mix"]), gate_col=3 * width, bm=512)

    mem_tokens = mem2d.shape[0] // batch
    kv, = _norm_proj(mem2d, row(p["g_mem"]), [p["w_kv_mem"].astype(BF16)], [None],
                     seq_len=mem_tokens, bm=min(512, mem2d.shape[0]), bn=1024)
    x2d = _cross_attn(x2d, kv, p["w_q_mem"].astype(BF16), p["w_o_mem"].astype(BF16),
                      row(p["g_pre_cross"]), row(p["g_post_cross"]),
                      seq_len=seq_len, mem_tokens=mem_tokens, n_heads=4, bm=512)

    return _ffn(x2d, p["w_ffn_in"].astype(BF16), p["w_ffn_out"].astype(BF16),
                row(p["g_pre_ffn"]), row(p["g_post_ffn"]), bm=512)


def kernel(x, mem, g_pre_mix, g_post_mix, w_in, shift_mix, decay_base, decay_up, iclr_base, iclr_up, gate_up, key_norm_scale, key_iclr_scale, bonus_scale, lnx_w, lnx_b, rel_bias, w_branch_a, w_branch_b, w_out, g_pre_cross, g_post_cross, g_mem, w_q_mem, w_kv_mem, w_o_mem, g_pre_ffn, g_post_ffn, w_ffn_in, w_ffn_out):
    batch, seq_len, d = x.shape
    stacked = dict(
        g_pre_mix=g_pre_mix, g_post_mix=g_post_mix, w_in=w_in, shift_mix=shift_mix,
        decay_base=decay_base, decay_up=decay_up, iclr_base=iclr_base, iclr_up=iclr_up,
        gate_up=gate_up, key_norm_scale=key_norm_scale, key_iclr_scale=key_iclr_scale,
        bonus_scale=bonus_scale, lnx_w=lnx_w, lnx_b=lnx_b, rel_bias=rel_bias,
        w_branch_a=w_branch_a, w_branch_b=w_branch_b, w_out=w_out, g_pre_cross=g_pre_cross,
        g_post_cross=g_post_cross, g_mem=g_mem, w_q_mem=w_q_mem, w_kv_mem=w_kv_mem,
        w_o_mem=w_o_mem, g_pre_ffn=g_pre_ffn, g_post_ffn=g_post_ffn, w_ffn_in=w_ffn_in,
        w_ffn_out=w_ffn_out)
    x2d = x.reshape(batch * seq_len, d)
    mem2d = mem.reshape(-1, d)
    for layer in range(g_pre_mix.shape[0]):
        x2d = _layer(x2d, mem2d, {k: v[layer] for k, v in stacked.items()},
                     batch=batch, seq_len=seq_len)
    return x2d.reshape(batch, seq_len, d)
```
